```python
import math
import jax, jax.numpy as jnp
from jax import lax
import numpy as np

D_MODEL = 1024
BATCH = 16
SEQ = 2048
DEPTH = 1

A_HEADS = 8
A_HEAD_DIM = 64
IDX_HEADS = 8
IDX_DIM = 64
TOPK_MAX = 256
Q_BLOCK = 128
ROPE_THETA = 10000.0
B_HEADS = 4
B_KEY_DIM = 128
B_VAL_DIM = 128
CONV_WIDTH = 4
CHUNK = 64
D_FF = 2816
MACARON_WEIGHT = 0.5
DEEPNORM_ALPHA = (2.0 * DEPTH) ** 0.25
DEEPNORM_BETA = (8.0 * DEPTH) ** -0.25
LN_EPS = 1e-5
RMS_EPS = 1e-6
N_MOD = 9

MIX_WIDTH = A_HEADS * A_HEAD_DIM + B_HEADS * B_VAL_DIM
IN_SPLITS = (A_HEADS * A_HEAD_DIM, A_HEAD_DIM, A_HEAD_DIM,
             IDX_HEADS * IDX_DIM, IDX_DIM, IDX_HEADS,
             B_HEADS * B_KEY_DIM, B_HEADS * B_KEY_DIM, B_HEADS * B_VAL_DIM,
             B_HEADS * B_VAL_DIM, B_HEADS, B_HEADS)
IN_WIDTH = sum(IN_SPLITS)
CONV_CH = 2 * B_HEADS * B_KEY_DIM + B_HEADS * B_VAL_DIM

kernel_name = "hymba_dsa_gdn_macaron_deepnorm_adaln"


def layer_norm(x, g, b):
    xf = x.astype(jnp.float32)
    mu = jnp.mean(xf, axis=-1, keepdims=True)
    var = jnp.mean(jnp.square(xf - mu), axis=-1, keepdims=True)
    return ((xf - mu) * lax.rsqrt(var + LN_EPS) * g + b).astype(x.dtype)


def rms_norm(x, g):
    xf = x.astype(jnp.float32)
    return xf * lax.rsqrt(jnp.mean(jnp.square(xf), axis=-1, keepdims=True) + RMS_EPS) * g


def l2norm(t):
    tf = t.astype(jnp.float32)
    return tf * lax.rsqrt(jnp.sum(jnp.square(tf), axis=-1, keepdims=True) + RMS_EPS)


def modulate(x, shift, scale):
    return x * (1.0 + scale) + shift


def swiglu_ffn(u, w1, w3, w2):
    return (jax.nn.silu(u @ w1) * (u @ w3)) @ w2


def rope(t, positions):
    d = t.shape[-1]
    inv_freq = ROPE_THETA ** (-jnp.arange(0, d, 2, dtype=jnp.float32) / d)
    ang = positions.astype(jnp.float32)[..., None] * inv_freq
    cos = jnp.cos(ang)[:, :, None, :]
    sin = jnp.sin(ang)[:, :, None, :]
    tf = t.astype(jnp.float32)
    t1, t2 = tf[..., : d // 2], tf[..., d // 2:]
    return jnp.concatenate([t1 * cos - t2 * sin, t2 * cos + t1 * sin], axis=-1).astype(t.dtype)


def dsa_sparse_attention(q, k, v, q_idx, k_idx, w_idx):
    bsz, seq = q.shape[0], q.shape[1]
    n_sel = min(TOPK_MAX, seq // 4)
    nblk = seq // Q_BLOCK
    key_pos = jnp.arange(seq)
    b_idx = jnp.arange(bsz)[:, None, None]

    def to_blocks(t):
        return jnp.moveaxis(t.reshape((bsz, nblk, Q_BLOCK) + t.shape[2:]), 1, 0)

    def block(args):
        blk, qb, qib, wib = args
        q_pos = blk * Q_BLOCK + jnp.arange(Q_BLOCK)
        causal = key_pos[None, :] <= q_pos[:, None]
        rel = jax.nn.relu(jnp.einsum('bqhd,bsd->bqhs', qib, k_idx))
        score = jnp.einsum('bqh,bqhs->bqs', wib, rel).astype(jnp.float32)
        score = jnp.where(causal[None], score, -jnp.inf)
        _, sel = lax.top_k(score, n_sel)
        valid = sel <= q_pos[None, :, None]
        k_sel = k[b_idx, sel]
        v_sel = v[b_idx, sel]
        logits = jnp.einsum('bqhd,bqnd->bqhn', qb, k_sel).astype(jnp.float32) * (A_HEAD_DIM ** -0.5)
        logits = jnp.where(valid[:, :, None, :], logits, -jnp.inf)
        p = jax.nn.softmax(logits, axis=-1).astype(v.dtype)
        return jnp.einsum('bqhn,bqnd->bqhd', p, v_sel)

    out = lax.map(block, (jnp.arange(nblk), to_blocks(q), to_blocks(q_idx), to_blocks(w_idx)))
    return jnp.moveaxis(out, 0, 1).reshape(bsz, seq, A_HEADS * A_HEAD_DIM)


def causal_short_conv(u, w):
    n_ch = u.shape[-1]
    return lax.conv_general_dilated(u, w[:, None, :].astype(u.dtype), window_strides=(1,),
                                    padding=[(CONV_WIDTH - 1, 0)],
                                    dimension_numbers=('NWC', 'WIO', 'NWC'),
                                    feature_group_count=n_ch)


def gated_delta_rule_chunked(q, k, v, log_a, beta):
    bsz, seq, nh, dk = q.shape
    dv = v.shape[-1]
    nc = seq // CHUNK

    def chunks(t):
        t = t.astype(jnp.float32).reshape((bsz, nc, CHUNK, nh) + t.shape[3:])
        return jnp.moveaxis(t, 3, 1)

    q, k, v, log_a, beta = (chunks(t) for t in (q, k, v, log_a, beta))
    q = q * (dk ** -0.5)
    gam = jnp.cumsum(log_a, axis=-1)
    diff = gam[..., :, None] - gam[..., None, :]
    strict = jnp.tril(jnp.ones((CHUNK, CHUNK), dtype=bool), -1)
    incl = jnp.tril(jnp.ones((CHUNK, CHUNK), dtype=bool))
    dec_strict = jnp.where(strict, jnp.exp(jnp.where(strict, diff, 0.0)), 0.0)
    dec_incl = jnp.where(incl, jnp.exp(jnp.where(incl, diff, 0.0)), 0.0)
    g_cum = jnp.exp(gam)
    lower = beta[..., :, None] * jnp.einsum('bhnid,bhnjd->bhnij', k, k) * dec_strict
    t_mat = lower + jnp.eye(CHUNK, dtype=jnp.float32)
    w = lax.linalg.triangular_solve(t_mat, (beta * g_cum)[..., None] * k,
                                    left_side=True, lower=True, unit_diagonal=True)
    u = lax.linalg.triangular_solve(t_mat, beta[..., None] * v,
                                    left_side=True, lower=True, unit_diagonal=True)
    a_qk = jnp.einsum('bhnid,bhnjd->bhnij', q, k) * dec_incl
    q_dec = q * g_cum[..., None]
    k_dec = k * jnp.exp(gam[..., -1:] - gam)[..., None]
    g_last = g_cum[..., -1]

    def step(state, xs):
        w_c, u_c, aqk_c, qd_c, kd_c, gl_c = xs
        delta = u_c - w_c @ state
        out = qd_c @ state + aqk_c @ delta
        state = gl_c[..., None, None] * state + jnp.swapaxes(kd_c, -1, -2) @ delta
        return state, out

    xs = tuple(jnp.moveaxis(t, 2, 0) for t in (w, u, a_qk, q_dec, k_dec, g_last))
    state0 = jnp.zeros((bsz, nh, dk, dv), jnp.float32)
    _, out = lax.scan(step, state0, xs)
    out = jnp.moveaxis(out, 0, 2)
    return jnp.moveaxis(out, 1, 3).reshape(bsz, seq, nh, dv)


def hybrid_mixer(u, positions, w_in, conv_w, a_log, dt_bias, dn_norm_g, w_out):
    bsz, seq, _ = u.shape
    split_points = tuple(int(s) for s in np.cumsum(IN_SPLITS)[:-1])
    proj = u @ w_in
    a_q, a_k, a_v, i_q, i_k, i_w, b_q, b_k, b_v, b_z, b_a, b_b = jnp.split(proj, split_points, axis=-1)

    q = rope(a_q.reshape(bsz, seq, A_HEADS, A_HEAD_DIM), positions)
    k = rope(a_k[:, :, None, :], positions)[:, :, 0]
    qi = rope(i_q.reshape(bsz, seq, IDX_HEADS, IDX_DIM), positions)
    ki = rope(i_k[:, :, None, :], positions)[:, :, 0]
    wi = i_w * (IDX_HEADS ** -0.5 * IDX_DIM ** -0.5)
    attn_out = dsa_sparse_attention(q, k, a_v, qi, ki, wi)

    qkv = jax.nn.silu(causal_short_conv(jnp.concatenate([b_q, b_k, b_v], axis=-1), conv_w))
    dq, dk_, dv_ = jnp.split(qkv, (B_HEADS * B_KEY_DIM, 2 * B_HEADS * B_KEY_DIM), axis=-1)
    dq = l2norm(dq.reshape(bsz, seq, B_HEADS, B_KEY_DIM))
    dk_ = l2norm(dk_.reshape(bsz, seq, B_HEADS, B_KEY_DIM))
    dv_ = dv_.reshape(bsz, seq, B_HEADS, B_VAL_DIM)
    log_a = -jnp.exp(a_log.astype(jnp.float32)) * jax.nn.softplus((b_a + dt_bias).astype(jnp.float32))
    beta = jax.nn.sigmoid(b_b.astype(jnp.float32))
    dn = gated_delta_rule_chunked(dq, dk_, dv_, log_a, beta)
    gate = jax.nn.silu(b_z.reshape(bsz, seq, B_HEADS, B_VAL_DIM).astype(jnp.float32))
    dn_out = (rms_norm(dn, dn_norm_g) * gate).astype(u.dtype).reshape(bsz, seq, B_HEADS * B_VAL_DIM)

    return jnp.concatenate([attn_out, dn_out], axis=-1) @ w_out


def setup_inputs(seed: int = 0) -> dict:
    key = jax.random.key(seed)
    ks = jax.random.split(key, 24)
    D = D_MODEL

    def nrm(k, shape, scale):
        return jax.random.normal(k, shape, jnp.float32) * scale

    x = nrm(ks[0], (BATCH, SEQ, D), 1.0)
    c = nrm(ks[1], (BATCH, D), 1.0)
    positions = (jnp.arange(SEQ, dtype=jnp.int32)[None, :]
                 + jax.random.randint(ks[2], (BATCH, 1), 0, 64, dtype=jnp.int32))
    w_ada = nrm(ks[3], (DEPTH, D, N_MOD * D), 0.5 * D ** -0.5)
    b_ada = nrm(ks[4], (DEPTH, N_MOD * D), 0.02)
    ffn1_w1 = nrm(ks[5], (DEPTH, D, D_FF), D ** -0.5)
    ffn1_w3 = nrm(ks[6], (DEPTH, D, D_FF), D ** -0.5)
    ffn1_w2 = nrm(ks[7], (DEPTH, D_FF, D), DEEPNORM_BETA * D_FF ** -0.5)
    ln1_g = 1.0 + nrm(ks[8], (DEPTH, D), 0.02)
    ln1_b = nrm(ks[9], (DEPTH, D), 0.02)
    w_in = nrm(ks[10], (DEPTH, D, IN_WIDTH), D ** -0.5)
    conv_w = nrm(ks[11], (DEPTH, CONV_WIDTH, CONV_CH), CONV_WIDTH ** -0.5)
    a_log = jnp.log(jax.random.uniform(ks[12], (DEPTH, B_HEADS), jnp.float32, 1.0, 16.0))
    dt = jnp.exp(jax.random.uniform(ks[13], (DEPTH, B_HEADS), jnp.float32,
                                    math.log(1e-3), math.log(1e-1)))
    dt_bias = dt + jnp.log(-jnp.expm1(-dt))
    dn_norm_g = 1.0 + nrm(ks[14], (DEPTH, B_VAL_DIM), 0.02)
    w_out = nrm(ks[15], (DEPTH, MIX_WIDTH, D), DEEPNORM_BETA * MIX_WIDTH ** -0.5)
    ln2_g = 1.0 + nrm(ks[16], (DEPTH, D), 0.02)
    ln2_b = nrm(ks[17], (DEPTH, D), 0.02)
    ffn2_w1 = nrm(ks[18], (DEPTH, D, D_FF), D ** -0.5)
    ffn2_w3 = nrm(ks[19], (DEPTH, D, D_FF), D ** -0.5)
    ffn2_w2 = nrm(ks[20], (DEPTH, D_FF, D), DEEPNORM_BETA * D_FF ** -0.5)
    ln3_g = 1.0 + nrm(ks[21], (DEPTH, D), 0.02)
    ln3_b = nrm(ks[22], (DEPTH, D), 0.02)
    return {'x': x, 'c': c, 'positions': positions, 'w_ada': w_ada, 'b_ada': b_ada,
            'ffn1_w1': ffn1_w1, 'ffn1_w3': ffn1_w3, 'ffn1_w2': ffn1_w2, 'ln1_g': ln1_g, 'ln1_b': ln1_b,
            'w_in': w_in, 'conv_w': conv_w, 'a_log': a_log, 'dt_bias': dt_bias, 'dn_norm_g': dn_norm_g,
            'w_out': w_out, 'ln2_g': ln2_g, 'ln2_b': ln2_b,
            'ffn2_w1': ffn2_w1, 'ffn2_w3': ffn2_w3, 'ffn2_w2': ffn2_w2, 'ln3_g': ln3_g, 'ln3_b': ln3_b}


def reference(x, c, positions, w_ada, b_ada, ffn1_w1, ffn1_w3, ffn1_w2, ln1_g, ln1_b,
              w_in, conv_w, a_log, dt_bias, dn_norm_g, w_out, ln2_g, ln2_b,
              ffn2_w1, ffn2_w3, ffn2_w2, ln3_g, ln3_b):
    for layer in range(DEPTH):
        mod = (jax.nn.silu(c) @ w_ada[layer] + b_ada[layer])[:, None, :]
        sh1, sc1, g1, sh2, sc2, g2, sh3, sc3, g3 = jnp.split(mod, N_MOD, axis=-1)
        h = swiglu_ffn(modulate(x, sh1, sc1), ffn1_w1[layer], ffn1_w3[layer], ffn1_w2[layer])
        x = layer_norm(DEEPNORM_ALPHA * x + MACARON_WEIGHT * g1 * h, ln1_g[layer], ln1_b[layer])
        h = hybrid_mixer(modulate(x, sh2, sc2), positions, w_in[layer], conv_w[layer], a_log[layer],
                         dt_bias[layer], dn_norm_g[layer], w_out[layer])
        x = layer_norm(DEEPNORM_ALPHA * x + g2 * h, ln2_g[layer], ln2_b[layer])
        h = swiglu_ffn(modulate(x, sh3, sc3), ffn2_w1[layer], ffn2_w3[layer], ffn2_w2[layer])
        x = layer_norm(DEEPNORM_ALPHA * x + MACARON_WEIGHT * g3 * h, ln3_g[layer], ln3_b[layer])
    return x
```

```python
import functools

import jax
import jax.numpy as jnp
from jax import lax
from jax.experimental import pallas as pl
from jax.experimental.pallas import tpu as pltpu

F32 = jnp.float32
BF16 = jnp.bfloat16

A_HEADS = 8
A_HEAD_DIM = 64
IDX_HEADS = 8
IDX_DIM = 64
TOPK_MAX = 256
Q_BLOCK = 128
ROPE_THETA = 10000.0
B_HEADS = 4
B_KEY_DIM = 128
B_VAL_DIM = 128
CONV_WIDTH = 4
MACARON_WEIGHT = 0.5
DEPTH = 1
DEEPNORM_ALPHA = (2.0 * DEPTH) ** 0.25
LN_EPS = 1e-5
RMS_EPS = 1e-6
N_MOD = 9

LANES = 128
SUBLANES = 8
VMEM_LIMIT_BYTES = 56 * 1024 * 1024

FFN_TOKENS = 1024
FFN_CHUNK = 256
PROJ_TOKENS = 512
OUT_TOKENS = 512
ROW_CHUNK = 256
KEY_CHUNK = 128
GDN_CHUNK = 256
GDN_HEADS_PER_STEP = 2
GDN_INV_BLOCK = 16

NEG_INF = float("-inf")


def _silu(x):
    return x * (1.0 / (1.0 + jnp.exp(-x)))


def _dot(a, b):
    return jnp.dot(a, b, preferred_element_type=F32)


def _dot_nt(a, b):
    return lax.dot_general(a, b, (((1,), (1,)), ((), ())), preferred_element_type=F32)


def _dot_tn(a, b):
    return lax.dot_general(a, b, (((0,), (0,)), ((), ())), preferred_element_type=F32)


def _layer_norm(y, g, b):
    mu = jnp.mean(y, axis=-1, keepdims=True)
    d = y - mu
    var = jnp.mean(d * d, axis=-1, keepdims=True)
    return d * lax.rsqrt(var + LN_EPS) * g + b


def _params(*sem):
    return pltpu.CompilerParams(dimension_semantics=sem, vmem_limit_bytes=VMEM_LIMIT_BYTES)


def _adaln_kernel(c_ref, w_ref, b_ref, o_ref):
    sc = _silu(c_ref[...]).astype(BF16)
    o_ref[...] = _dot(sc, w_ref[...].astype(BF16)) + b_ref[...]


def _adaln(c, w, b):
    bsz, d = c.shape
    n = w.shape[1]
    tn = n // 8
    return pl.pallas_call(
        _adaln_kernel,
        grid=(n // tn,),
        in_specs=[pl.BlockSpec((bsz, d), lambda j: (0, 0)),
                  pl.BlockSpec((d, tn), lambda j: (0, j)),
                  pl.BlockSpec((1, tn), lambda j: (0, j))],
        out_specs=pl.BlockSpec((bsz, tn), lambda j: (0, j)),
        out_shape=jax.ShapeDtypeStruct((bsz, n), F32),
        compiler_params=_params("arbitrary"),
        name="adaln",
    )(c, w, b.reshape(1, n))


def _ffn_kernel(x_ref, mod_ref, w1_ref, w3_ref, w2_ref, g_ref, b_ref, o_ref, u_ref, acc_ref,
                *, mod_base, n_chunks, n_row_chunks):
    shift = mod_ref[0, mod_base:mod_base + 1, :]
    scale = mod_ref[0, mod_base + 1:mod_base + 2, :]
    gate = mod_ref[0, mod_base + 2:mod_base + 3, :]

    def prologue(r, carry):
        rows = pl.ds(pl.multiple_of(r * ROW_CHUNK, ROW_CHUNK), ROW_CHUNK)
        u_ref[rows, :] = (x_ref[0, rows, :] * (1.0 + scale) + shift).astype(BF16)
        return carry

    lax.fori_loop(0, n_row_chunks, prologue, 0)

    def chunk(f, carry):
        u = u_ref[...]
        a = _dot(u, w1_ref[f])
        b = _dot(u, w3_ref[f])
        hid = (_silu(a) * b).astype(BF16)
        contrib = _dot(hid, w2_ref[f])

        @pl.when(f == 0)
        def _():
            acc_ref[...] = contrib

        @pl.when(f > 0)
        def _():
            acc_ref[...] += contrib

        return carry

    lax.fori_loop(0, n_chunks, chunk, 0)

    def epilogue(r, carry):
        rows = pl.ds(pl.multiple_of(r * ROW_CHUNK, ROW_CHUNK), ROW_CHUNK)
        y = DEEPNORM_ALPHA * x_ref[0, rows, :] + (MACARON_WEIGHT * gate) * acc_ref[rows, :]
        o_ref[0, rows, :] = _layer_norm(y, g_ref[...], b_ref[...])
        return carry

    lax.fori_loop(0, n_row_chunks, epilogue, 0)


def _ffn_block(x, mod, w1, w3, w2, ln_g, ln_b, *, mod_base):
    bsz, seq, d = x.shape
    dff = w1.shape[1]
    n_chunks = dff // FFN_CHUNK
    tm = min(FFN_TOKENS, seq)
    w1c = w1.astype(BF16).reshape(d, n_chunks, FFN_CHUNK).transpose(1, 0, 2)
    w3c = w3.astype(BF16).reshape(d, n_chunks, FFN_CHUNK).transpose(1, 0, 2)
    w2c = w2.astype(BF16).reshape(n_chunks, FFN_CHUNK, d)
    kern = functools.partial(_ffn_kernel, mod_base=mod_base, n_chunks=n_chunks,
                             n_row_chunks=tm // ROW_CHUNK)
    const3 = lambda b, i: (0, 0, 0)
    return pl.pallas_call(
        kern,
        grid=(bsz, seq // tm),
        in_specs=[pl.BlockSpec((1, tm, d), lambda b, i: (b, i, 0)),
                  pl.BlockSpec((1, N_MOD, d), lambda b, i: (b, 0, 0)),
                  pl.BlockSpec((n_chunks, d, FFN_CHUNK), const3, pipeline_mode=pl.Buffered(1)),
                  pl.BlockSpec((n_chunks, d, FFN_CHUNK), const3, pipeline_mode=pl.Buffered(1)),
                  pl.BlockSpec((n_chunks, FFN_CHUNK, d), const3, pipeline_mode=pl.Buffered(1)),
                  pl.BlockSpec((1, d), lambda b, i: (0, 0)),
                  pl.BlockSpec((1, d), lambda b, i: (0, 0))],
        out_specs=pl.BlockSpec((1, tm, d), lambda b, i: (b, i, 0)),
        out_shape=jax.ShapeDtypeStruct((bsz, seq, d), F32),
        scratch_shapes=[pltpu.VMEM((tm, d), BF16), pltpu.VMEM((tm, d), F32)],
        compiler_params=_params("arbitrary", "arbitrary"),
        name="ffn",
    )(x, mod, w1c, w3c, w2c, ln_g.reshape(1, d), ln_b.reshape(1, d))


_ROW_Q = 0
_ROW_QI = _ROW_Q + A_HEADS * A_HEAD_DIM
_ROW_K = _ROW_QI + IDX_HEADS * IDX_DIM
_ROW_KI = _ROW_K + A_HEAD_DIM
_ROW_V = _ROW_KI + IDX_DIM
_ROW_W = _ROW_V + A_HEAD_DIM
_ROWS_A = _ROW_W + IDX_HEADS
_ROWS_A_PADDED = -(-_ROWS_A // 16) * 16
_COL_QKV = 0
_COL_Z = 2 * B_HEADS * B_KEY_DIM + B_HEADS * B_VAL_DIM
_COL_AB = _COL_Z + B_HEADS * B_VAL_DIM
_COLS_B = _COL_AB + LANES


def _proj_kernel(x_ref, mod_ref, pos_ref, invf_ref, wat_ref, wb_ref,
                 qt_ref, qit_ref, wt_ref, k_ref, ki_ref, vt_ref, qkv_ref, z_ref, ab_ref, u_ref,
                 *, n_row_chunks):
    shift = mod_ref[0, 3:4, :]
    scale = mod_ref[0, 4:5, :]

    def prologue(r, carry):
        rows = pl.ds(pl.multiple_of(r * ROW_CHUNK, ROW_CHUNK), ROW_CHUNK)
        u_ref[rows, :] = (x_ref[0, rows, :] * (1.0 + scale) + shift).astype(BF16)
        return carry

    lax.fori_loop(0, n_row_chunks, prologue, 0)
    u = u_ref[...]

    qkv_ref[0] = _dot(u, wb_ref[:, _COL_QKV:_COL_Z])
    z_ref[0] = _dot(u, wb_ref[:, _COL_Z:_COL_AB])
    ab_ref[0] = _dot(u, wb_ref[:, _COL_AB:_COLS_B])

    ang = invf_ref[...] * pos_ref[0]
    cos = jnp.cos(ang)
    sin = jnp.sin(ang)
    half = A_HEAD_DIM // 2

    def roped(row0):
        t = _dot_nt(wat_ref[row0:row0 + A_HEAD_DIM, :], u)
        t1 = t[:half]
        t2 = t[half:]
        return t1 * cos - t2 * sin, t2 * cos + t1 * sin

    for h in range(A_HEADS):
        o1, o2 = roped(_ROW_Q + h * A_HEAD_DIM)
        qt_ref[0, h * A_HEAD_DIM:h * A_HEAD_DIM + half, :] = (o1 * (A_HEAD_DIM ** -0.5)).astype(BF16)
        qt_ref[0, h * A_HEAD_DIM + half:(h + 1) * A_HEAD_DIM, :] = (o2 * (A_HEAD_DIM ** -0.5)).astype(BF16)
    for h in range(IDX_HEADS):
        o1, o2 = roped(_ROW_QI + h * IDX_DIM)
        qit_ref[0, h * IDX_DIM:h * IDX_DIM + half, :] = o1.astype(BF16)
        qit_ref[0, h * IDX_DIM + half:(h + 1) * IDX_DIM, :] = o2.astype(BF16)

    k1, k2 = roped(_ROW_K)
    ki1, ki2 = roped(_ROW_KI)
    kk = jnp.concatenate([k1, k2, ki1, ki2], axis=0).T
    k_ref[0] = kk[:, :A_HEAD_DIM].astype(BF16)
    ki_ref[0] = kk[:, A_HEAD_DIM:].astype(BF16)

    vt = _dot_nt(wat_ref[_ROW_V:_ROW_V + A_HEAD_DIM, :], u).astype(BF16)
    for j in range(vt_ref.shape[1]):
        vt_ref[0, j] = vt[:, j * KEY_CHUNK:(j + 1) * KEY_CHUNK]

    wt = _dot_nt(wat_ref[_ROW_W:_ROWS_A_PADDED, :], u)
    wt_ref[0] = wt[:IDX_HEADS] * (IDX_HEADS ** -0.5 * IDX_DIM ** -0.5)


def _proj(x, mod, pos_f, w_in):
    bsz, seq, d = x.shape
    tm = min(PROJ_TOKENS, seq)
    splits = (A_HEADS * A_HEAD_DIM, A_HEAD_DIM, A_HEAD_DIM, IDX_HEADS * IDX_DIM, IDX_DIM, IDX_HEADS,
              B_HEADS * B_KEY_DIM, B_HEADS * B_KEY_DIM, B_HEADS * B_VAL_DIM, B_HEADS * B_VAL_DIM,
              B_HEADS, B_HEADS)
    offs = [0]
    for s in splits:
        offs.append(offs[-1] + s)
    cols = [w_in[:, offs[i]:offs[i + 1]] for i in range(len(splits))]
    a_q, a_k, a_v, i_q, i_k, i_w, b_q, b_k, b_v, b_z, b_a, b_b = cols
    wat = jnp.concatenate([a_q, i_q, a_k, i_k, a_v, i_w], axis=1).T.astype(BF16)
    wat = jnp.pad(wat, ((0, _ROWS_A_PADDED - _ROWS_A), (0, 0)))
    wb = jnp.concatenate([b_q, b_k, b_v, b_z, b_a, b_b], axis=1).astype(BF16)
    wb = jnp.pad(wb, ((0, 0), (0, _COLS_B - wb.shape[1])))
    half = A_HEAD_DIM // 2
    inv_freq = (ROPE_THETA ** (-jnp.arange(0, A_HEAD_DIM, 2, dtype=F32) / A_HEAD_DIM)).reshape(half, 1)

    n_kc = tm // KEY_CHUNK
    qkv_w = _COL_Z
    z_w = _COL_AB - _COL_Z
    tok = lambda b, i: (b, i, 0)
    tok_t = lambda b, i: (b, 0, i)
    const2 = lambda b, i: (0, 0)
    out_shapes = (
        jax.ShapeDtypeStruct((bsz, A_HEADS * A_HEAD_DIM, seq), BF16),
        jax.ShapeDtypeStruct((bsz, IDX_HEADS * IDX_DIM, seq), BF16),
        jax.ShapeDtypeStruct((bsz, IDX_HEADS, seq), F32),
        jax.ShapeDtypeStruct((bsz, seq, A_HEAD_DIM), BF16),
        jax.ShapeDtypeStruct((bsz, seq, IDX_DIM), BF16),
        jax.ShapeDtypeStruct((bsz, seq // KEY_CHUNK, A_HEAD_DIM, KEY_CHUNK), BF16),
        jax.ShapeDtypeStruct((bsz, seq, qkv_w), F32),
        jax.ShapeDtypeStruct((bsz, seq, z_w), F32),
        jax.ShapeDtypeStruct((bsz, seq, LANES), F32),
    )
    out_specs = (
        pl.BlockSpec((1, A_HEADS * A_HEAD_DIM, tm), tok_t),
        pl.BlockSpec((1, IDX_HEADS * IDX_DIM, tm), tok_t),
        pl.BlockSpec((1, IDX_HEADS, tm), tok_t),
        pl.BlockSpec((1, tm, A_HEAD_DIM), tok),
        pl.BlockSpec((1, tm, IDX_DIM), tok),
        pl.BlockSpec((1, n_kc, A_HEAD_DIM, KEY_CHUNK), lambda b, i: (b, i, 0, 0)),
        pl.BlockSpec((1, tm, qkv_w), tok),
        pl.BlockSpec((1, tm, z_w), tok),
        pl.BlockSpec((1, tm, LANES), tok),
    )
    kern = functools.partial(_proj_kernel, n_row_chunks=tm // ROW_CHUNK)
    return pl.pallas_call(
        kern,
        grid=(bsz, seq // tm),
        in_specs=[pl.BlockSpec((1, tm, d), tok),
                  pl.BlockSpec((1, N_MOD, d), lambda b, i: (b, 0, 0)),
                  pl.BlockSpec((1, 1, tm), tok_t),
                  pl.BlockSpec((half, 1), const2),
                  pl.BlockSpec((_ROWS_A_PADDED, d), const2),
                  pl.BlockSpec((d, _COLS_B), const2)],
        out_specs=out_specs,
        out_shape=out_shapes,
        scratch_shapes=[pltpu.VMEM((tm, d), BF16)],
        compiler_params=_params("arbitrary", "arbitrary"),
        name="proj",
    )(x, mod, pos_f, inv_freq, wat, wb)


_KEY_LO = -2139095041
_KEY_HI = 2139095041


def _key_to_float(key):
    bits = jnp.where(key < 0, key ^ 0x7FFFFFFF, key)
    return lax.bitcast_convert_type(bits, F32)


def _fold_rows(t):
    return t.reshape(t.shape[0] // SUBLANES, SUBLANES, t.shape[1])


def _dsa_kernel(qt_ref, qit_ref, wt_ref, k_ref, ki_ref, vt_ref, o_ref, s_ref, lg_ref, ot_ref,
                *, n_sel, search_steps):
    blk = pl.program_id(1)
    n_kc = blk + 1
    kc = KEY_CHUNK
    row = lax.broadcasted_iota(jnp.int32, (kc, Q_BLOCK), 0)
    col = lax.broadcasted_iota(jnp.int32, (kc, Q_BLOCK), 1)
    wt = wt_ref[0]

    def score_chunk(c, carry):
        kic = ki_ref[0, pl.ds(pl.multiple_of(c * kc, kc), kc), :]
        acc = jnp.zeros((kc, Q_BLOCK), F32)
        for h in range(IDX_HEADS):
            xh = _dot(kic, qit_ref[0, h * IDX_DIM:(h + 1) * IDX_DIM, :])
            acc = acc + wt[h:h + 1, :] * jnp.maximum(xh, 0.0)
        causal = row + (c - blk) * kc <= col
        s_ref[c] = jnp.where(causal, acc, NEG_INF)
        return carry

    lax.fori_loop(0, n_kc, score_chunk, 0)

    def count(pred):
        def body(c, acc):
            ind = jnp.where(pred(s_ref[c]), 1.0, 0.0)
            return acc + jnp.sum(_fold_rows(ind), axis=0)
        acc = lax.fori_loop(0, n_kc, body, jnp.zeros((SUBLANES, Q_BLOCK), F32))
        return jnp.sum(acc, axis=0, keepdims=True)

    @pl.when(n_kc * kc <= n_sel)
    def _():
        def body(c, carry):
            s_ref[c] = jnp.where(s_ref[c] > NEG_INF, 0.0, NEG_INF)
            return carry
        lax.fori_loop(0, n_kc, body, 0)

    @pl.when(n_kc * kc > n_sel)
    def _():
        def bisect(i, carry):
            lo, hi, cnt_lo = carry
            mid = (lo & hi) + ((lo ^ hi) >> 1)
            thr = _key_to_float(mid)
            cnt = count(lambda s: s >= thr)
            ok = cnt >= n_sel
            return jnp.where(ok, mid, lo), jnp.where(ok, hi, mid), jnp.where(ok, cnt, cnt_lo)

        lo0 = jnp.full((1, Q_BLOCK), _KEY_LO, jnp.int32)
        hi0 = jnp.full((1, Q_BLOCK), _KEY_HI, jnp.int32)
        cnt0 = jnp.full((1, Q_BLOCK), float(n_sel), F32)
        lo, _, cnt_lo = lax.fori_loop(0, search_steps, bisect, (lo0, hi0, cnt0))
        thr = _key_to_float(lo)
        has_ties = jnp.max(cnt_lo) > n_sel

        @pl.when(jnp.logical_not(has_ties))
        def _():
            def body(c, carry):
                s_ref[c] = jnp.where(s_ref[c] >= thr, 0.0, NEG_INF)
                return carry
            lax.fori_loop(0, n_kc, body, 0)

        @pl.when(has_ties)
        def _():
            quota = n_sel - count(lambda s: s > thr)
            lower = (lax.broadcasted_iota(jnp.int32, (kc, kc), 1)
                     < lax.broadcasted_iota(jnp.int32, (kc, kc), 0))
            lower = jnp.where(lower, 1.0, 0.0).astype(BF16)

            def body(c, seen):
                s = s_ref[c]
                eq = jnp.where(s == thr, 1.0, 0.0)
                rank = _dot(lower, eq.astype(BF16)) + seen
                take = jnp.where(s > thr, 1.0, jnp.where(rank < quota, eq, 0.0))
                s_ref[c] = jnp.where(take > 0.5, 0.0, NEG_INF)
                return seen + jnp.sum(eq, axis=0, keepdims=True)

            lax.fori_loop(0, n_kc, body, jnp.zeros((1, Q_BLOCK), F32))

    def logits_chunk(c, m):
        kch = k_ref[0, pl.ds(pl.multiple_of(c * kc, kc), kc), :]
        bias = s_ref[c]
        new_m = []
        for h in range(A_HEADS):
            lg = _dot(kch, qt_ref[0, h * A_HEAD_DIM:(h + 1) * A_HEAD_DIM, :]) + bias
            lg_ref[c, h] = lg
            new_m.append(jnp.maximum(m[h], jnp.max(_fold_rows(lg), axis=0)))
        return tuple(new_m)

    m0 = tuple(jnp.full((SUBLANES, Q_BLOCK), NEG_INF, F32) for _ in range(A_HEADS))
    m = lax.fori_loop(0, n_kc, logits_chunk, m0)
    m = tuple(jnp.max(mh, axis=0, keepdims=True) for mh in m)

    ot_ref[...] = jnp.zeros(ot_ref.shape, F32)

    def pv_chunk(c, l):
        vch = vt_ref[0, c]
        new_l = []
        for h in range(A_HEADS):
            p = jnp.exp(lg_ref[c, h] - m[h])
            new_l.append(l[h] + jnp.sum(_fold_rows(p), axis=0))
            ot_ref[h * A_HEAD_DIM:(h + 1) * A_HEAD_DIM, :] += _dot(vch, p.astype(BF16))
        return tuple(new_l)

    l0 = tuple(jnp.zeros((SUBLANES, Q_BLOCK), F32) for _ in range(A_HEADS))
    l = lax.fori_loop(0, n_kc, pv_chunk, l0)
    for h in range(A_HEADS):
        denom = jnp.sum(l[h], axis=0, keepdims=True)
        rows = slice(h * A_HEAD_DIM, (h + 1) * A_HEAD_DIM)
        ot_ref[rows, :] = ot_ref[rows, :] / denom
    o_ref[0] = ot_ref[...].T.astype(BF16)


def _dsa(qt, qit, wt, k, ki, vt):
    bsz, _, seq = qt.shape
    n_sel = min(TOPK_MAX, seq // 4)
    n_blk = seq // Q_BLOCK
    n_kc = seq // KEY_CHUNK
    hd = A_HEADS * A_HEAD_DIM
    kern = functools.partial(_dsa_kernel, n_sel=n_sel, search_steps=32)
    return pl.pallas_call(
        kern,
        grid=(bsz, n_blk),
        in_specs=[pl.BlockSpec((1, hd, Q_BLOCK), lambda b, i: (b, 0, i)),
                  pl.BlockSpec((1, IDX_HEADS * IDX_DIM, Q_BLOCK), lambda b, i: (b, 0, i)),
                  pl.BlockSpec((1, IDX_HEADS, Q_BLOCK), lambda b, i: (b, 0, i)),
                  pl.BlockSpec((1, seq, A_HEAD_DIM), lambda b, i: (b, 0, 0)),
                  pl.BlockSpec((1, seq, IDX_DIM), lambda b, i: (b, 0, 0)),
                  pl.BlockSpec((1, n_kc, A_HEAD_DIM, KEY_CHUNK), lambda b, i: (b, 0, 0, 0))],
        out_specs=pl.BlockSpec((1, Q_BLOCK, hd), lambda b, i: (b, i, 0)),
        out_shape=jax.ShapeDtypeStruct((bsz, seq, hd), BF16),
        scratch_shapes=[pltpu.VMEM((n_kc, KEY_CHUNK, Q_BLOCK), F32),
                        pltpu.VMEM((n_kc, A_HEADS, KEY_CHUNK, Q_BLOCK), F32),
                        pltpu.VMEM((hd, Q_BLOCK), F32)],
        compiler_params=_params("arbitrary", "arbitrary"),
        name="dsa",
    )(qt, qit, wt, k, ki, vt)


def _softplus(x):
    return jnp.maximum(x, 0.0) + jnp.log(1.0 + jnp.exp(-jnp.abs(x)))


def _split3(x):
    hi = x.astype(BF16)
    r = x - hi.astype(F32)
    mid = r.astype(BF16)
    lo = (r - mid.astype(F32)).astype(BF16)
    return hi, mid, lo


def _gdn_kernel(q_ref, k_ref, v_ref, z_ref, ab_ref, cwq_ref, cwk_ref, cwv_ref, alog_ref, dtb_ref, ng_ref,
                o_ref, qn_ref, kn_ref, vv_ref, st_ref, *, n_steps, heads):
    c = GDN_CHUNK
    dk = B_KEY_DIM
    halo = SUBLANES
    head0 = pl.program_id(1) * heads

    def conv_chunk(s, carry):
        r0 = pl.multiple_of(s * c, c)
        prev0 = pl.multiple_of(jnp.maximum(r0 - halo, 0), halo)
        live = jnp.where(s > 0, 1.0, 0.0)

        def conv(src_ref, w_ref):
            cur = src_ref[0, pl.ds(r0, c), :]
            prev = src_ref[0, pl.ds(prev0, halo), :] * live
            xw = jnp.concatenate([prev, cur], axis=0)
            y = xw[halo:] * w_ref[CONV_WIDTH - 1:CONV_WIDTH, :]
            for j in range(CONV_WIDTH - 1):
                sh = CONV_WIDTH - 1 - j
                y = y + xw[halo - sh:halo - sh + c] * w_ref[j:j + 1, :]
            return _silu(y)

        def l2n(t):
            parts = []
            for h in range(heads):
                th = t[:, h * dk:(h + 1) * dk]
                parts.append(th * lax.rsqrt(jnp.sum(th * th, axis=-1, keepdims=True) + RMS_EPS))
            return jnp.concatenate(parts, axis=1)

        rows = pl.ds(r0, c)
        qn_ref[rows, :] = l2n(conv(q_ref, cwq_ref))
        kn_ref[rows, :] = l2n(conv(k_ref, cwk_ref))
        vv_ref[rows, :] = conv(v_ref, cwv_ref)
        return carry

    lax.fori_loop(0, n_steps, conv_chunk, 0)

    st_ref[...] = jnp.zeros(st_ref.shape, F32)
    ri = lax.broadcasted_iota(jnp.int32, (c, c), 0)
    ci = lax.broadcasted_iota(jnp.int32, (c, c), 1)
    strict = ri > ci
    eye = jnp.where(ri == ci, 1.0, 0.0)
    incl_ones = jnp.where(ri >= ci, 1.0, 0.0).astype(BF16)
    inv_shift = GDN_INV_BLOCK.bit_length() - 1
    diag_mask = jnp.where((ri >> inv_shift) == (ci >> inv_shift), 1.0, 0.0)
    level_masks = []
    for ls in range(inv_shift, c.bit_length() - 1):
        lower_left = jnp.where((ri >> ls) == (ci >> ls) + 1, 1.0, 0.0) * jnp.where(((ci >> ls) & 1) == 0, 1.0, 0.0)
        level_masks.append(lower_left)
    lane =lax.broadcasted_iota(jnp.int32, (c, LANES), 1)

    def step(s, carry):
        rows = pl.ds(pl.multiple_of(s * c, c), c)
        ab = ab_ref[0, rows, :]
        log_a = -jnp.exp(alog_ref[...]) * _softplus(ab + dtb_ref[...])
        beta_all = 1.0 / (1.0 + jnp.exp(-ab))
        la_hi, la_mid, la_lo = _split3(log_a)
        gam_all = _dot(incl_ones, la_hi) + _dot(incl_ones, la_mid) + _dot(incl_ones, la_lo)

        for h in range(heads):
            hsel = lane == head0 + h
            bsel = lane == head0 + h + B_HEADS
            gam = jnp.sum(jnp.where(hsel, gam_all, 0.0), axis=1, keepdims=True)
            beta = jnp.sum(jnp.where(bsel, beta_all, 0.0), axis=1, keepdims=True)
            cols = slice(h * dk, (h + 1) * dk)
            kh = kn_ref[rows, cols]
            qh = qn_ref[rows, cols] * (dk ** -0.5)
            vh = vv_ref[rows, cols]

            gcol = jnp.broadcast_to(gam, (c, c))
            diff = gcol - gcol.T
            dec = jnp.where(strict, jnp.exp(jnp.where(strict, diff, 0.0)), 0.0)
            kb = kh.astype(BF16)
            kk = _dot_nt(kb, kb)
            qk = _dot_nt(qh.astype(BF16), kb)
            nmat = beta * kk * dec
            a_qk = qk * (dec + eye)

            mp = -(nmat * diag_mask)
            x = eye + mp
            for _ in range(GDN_INV_BLOCK.bit_length() - 2):
                mpb = mp.astype(BF16)
                mp = _dot(mpb, mpb)
                x = x + _dot(x.astype(BF16), mp.astype(BF16))
            for lm in level_masks:
                xb = x.astype(BF16)
                x = x - _dot(xb, _dot((nmat * lm).astype(BF16), xb).astype(BF16))
            xb = x.astype(BF16)

            g_cum = jnp.exp(gam)
            w = _dot(xb, ((beta * g_cum) * kh).astype(BF16))
            u = _dot(xb, (beta * vh).astype(BF16))
            g_last = gam[c - 1:c, :]
            q_dec = (qh * g_cum).astype(BF16)
            k_dec = (kh * jnp.exp(g_last - gam)).astype(BF16)

            state = st_ref[h]
            sb = state.astype(BF16)
            delta = u - _dot(w.astype(BF16), sb)
            db = delta.astype(BF16)
            out = _dot(q_dec, sb) + _dot(a_qk.astype(BF16), db)
            st_ref[h] = jnp.exp(g_last) * state + _dot_tn(k_dec, db)

            normed = out * lax.rsqrt(jnp.mean(out * out, axis=-1, keepdims=True) + RMS_EPS) * ng_ref[...]
            o_ref[0, rows, cols] = (normed * _silu(z_ref[0, rows, cols])).astype(BF16)
        return carry

    lax.fori_loop(0, n_steps, step, 0)


def _gdn(qkv, z, ab, conv_w, a_log, dt_bias, norm_g):
    bsz, seq, _ = qkv.shape
    heads = GDN_HEADS_PER_STEP
    n_groups = B_HEADS // heads
    gw = heads * B_KEY_DIM
    pad_lanes = lambda v: jnp.pad(v.reshape(1, -1), ((0, 0), (0, LANES - v.size)))
    kern = functools.partial(_gdn_kernel, n_steps=seq // GDN_CHUNK, heads=heads)
    col_block = lambda off: (lambda b, g: (b, 0, off + g))
    w_block = lambda off: (lambda b, g: (0, off + g))
    const2 = lambda b, g: (0, 0)
    return pl.pallas_call(
        kern,
        grid=(bsz, n_groups),
        in_specs=[pl.BlockSpec((1, seq, gw), col_block(0)),
                  pl.BlockSpec((1, seq, gw), col_block(n_groups)),
                  pl.BlockSpec((1, seq, gw), col_block(2 * n_groups)),
                  pl.BlockSpec((1, seq, gw), col_block(0)),
                  pl.BlockSpec((1, seq, LANES), lambda b, g: (b, 0, 0)),
                  pl.BlockSpec((CONV_WIDTH, gw), w_block(0)),
                  pl.BlockSpec((CONV_WIDTH, gw), w_block(n_groups)),
                  pl.BlockSpec((CONV_WIDTH, gw), w_block(2 * n_groups)),
                  pl.BlockSpec((1, LANES), const2),
                  pl.BlockSpec((1, LANES), const2),
                  pl.BlockSpec((1, B_VAL_DIM), const2)],
        out_specs=pl.BlockSpec((1, seq, gw), col_block(0)),
        out_shape=jax.ShapeDtypeStruct((bsz, seq, B_HEADS * B_VAL_DIM), BF16),
        scratch_shapes=[pltpu.VMEM((seq, gw), F32), pltpu.VMEM((seq, gw), F32), pltpu.VMEM((seq, gw), F32),
                        pltpu.VMEM((heads, B_KEY_DIM, B_VAL_DIM), F32)],
        compiler_params=_params("arbitrary", "arbitrary"),
        name="gdn",
    )(qkv, qkv, qkv, z, ab, conv_w, conv_w, conv_w, pad_lanes(a_log), pad_lanes(dt_bias),
      norm_g.reshape(1, B_VAL_DIM))


def _outproj_kernel(x_ref, mod_ref, attn_ref, dn_ref, wo_ref, g_ref, b_ref, o_ref, *, n_row_chunks):
    gate = mod_ref[0, 5:6, :]
    n_attn = attn_ref.shape[2]

    def body(r, carry):
        rows = pl.ds(pl.multiple_of(r * ROW_CHUNK, ROW_CHUNK), ROW_CHUNK)
        h = _dot(attn_ref[0, rows, :], wo_ref[:n_attn, :]) + _dot(dn_ref[0, rows, :], wo_ref[n_attn:, :])
        y = DEEPNORM_ALPHA * x_ref[0, rows, :] + gate * h
        o_ref[0, rows, :] = _layer_norm(y, g_ref[...], b_ref[...])
        return carry

    lax.fori_loop(0, n_row_chunks, body, 0)


def _outproj(x, mod, attn, dn, w_out, ln_g, ln_b):
    bsz, seq, d = x.shape
    tm = min(OUT_TOKENS, seq)
    tok = lambda b, i: (b, i, 0)
    const2 = lambda b, i: (0, 0)
    kern = functools.partial(_outproj_kernel, n_row_chunks=tm // ROW_CHUNK)
    return pl.pallas_call(
        kern,
        grid=(bsz, seq // tm),
        in_specs=[pl.BlockSpec((1, tm, d), tok),
                  pl.BlockSpec((1, N_MOD, d), lambda b, i: (b, 0, 0)),
                  pl.BlockSpec((1, tm, attn.shape[2]), tok),
                  pl.BlockSpec((1, tm, dn.shape[2]), tok),
                  pl.BlockSpec(w_out.shape, const2),
                  pl.BlockSpec((1, d), const2),
                  pl.BlockSpec((1, d), const2)],
        out_specs=pl.BlockSpec((1, tm, d), tok),
        out_shape=jax.ShapeDtypeStruct((bsz, seq, d), F32),
        compiler_params=_params("arbitrary", "arbitrary"),
        name="outproj",
    )(x, mod, attn, dn, w_out.astype(BF16), ln_g.reshape(1, d), ln_b.reshape(1, d))


def kernel(x, c, positions, w_ada, b_ada, ffn1_w1, ffn1_w3, ffn1_w2, ln1_g, ln1_b, w_in, conv_w, a_log, dt_bias, dn_norm_g, w_out, ln2_g, ln2_b, ffn2_w1, ffn2_w3, ffn2_w2, ln3_g, ln3_b):
    bsz, seq, d = x.shape
    pos_f = positions.astype(F32).reshape(bsz, 1, seq)
    for layer in range(w_ada.shape[0]):
        mod = _adaln(c, w_ada[layer], b_ada[layer]).reshape(bsz, N_MOD, d)
        x = _ffn_block(x, mod, ffn1_w1[layer], ffn1_w3[layer], ffn1_w2[layer], ln1_g[layer], ln1_b[layer],
                       mod_base=0)
        qt, qit, wt, k, ki, vt, qkv, z, ab = _proj(x, mod, pos_f, w_in[layer])
        attn = _dsa(qt, qit, wt, k, ki, vt)
        dn = _gdn(qkv, z, ab, conv_w[layer], a_log[layer], dt_bias[layer], dn_norm_g[layer])
        x = _outproj(x, mod, attn, dn, w_out[layer], ln2_g[layer], ln2_b[layer])
        x = _ffn_block(x, mod, ffn2_w1[layer], ffn2_w3[layer], ffn2_w2[layer], ln3_g[layer], ln3_b[layer],
                       mod_base=6)
    return x
```

```python
import functools

import jax
import jax.numpy as jnp
from jax import lax
from jax.experimental import pallas as pl
from jax.experimental.pallas import tpu as pltpu

F32 = jnp.float32
BF16 = jnp.bfloat16

A_HEADS = 8
A_HEAD_DIM = 64
IDX_HEADS = 8
IDX_DIM = 64
TOPK_MAX = 256
ROPE_THETA = 10000.0
B_HEADS = 4
B_KEY_DIM = 128
B_VAL_DIM = 128
CONV_WIDTH = 4
MACARON_WEIGHT = 0.5
DEPTH = 1
DEEPNORM_ALPHA = (2.0 * DEPTH) ** 0.25
LN_EPS = 1e-5
RMS_EPS = 1e-6
N_MOD = 9

LANES = 128
SUBLANES = 8
VMEM_LIMIT_BYTES = 56 * 1024 * 1024

FFN_TOKENS = 1024
FFN_CHUNK = 256
PROJ_TOKENS = 512
OUT_TOKENS = 512
ROW_CHUNK = 256
Q_BLOCK = 256
KEY_CHUNK = 256
GDN_CHUNK = 256
GDN_HEADS_PER_STEP = 2
GDN_INV_BLOCK = 16

NEG_INF = float("-inf")


def _silu(x):
    return x * (1.0 / (1.0 + jnp.exp(-x)))


def _dot(a, b):
    return jnp.dot(a, b, preferred_element_type=F32)


def _dot_nt(a, b):
    return lax.dot_general(a, b, (((1,), (1,)), ((), ())), preferred_element_type=F32)


def _dot_tn(a, b):
    return lax.dot_general(a, b, (((0,), (0,)), ((), ())), preferred_element_type=F32)


def _layer_norm(y, g, b):
    mu = jnp.mean(y, axis=-1, keepdims=True)
    d = y - mu
    var = jnp.mean(d * d, axis=-1, keepdims=True)
    return d * lax.rsqrt(var + LN_EPS) * g + b


def _params(*sem):
    return pltpu.CompilerParams(dimension_semantics=sem, vmem_limit_bytes=VMEM_LIMIT_BYTES)


def _adaln_kernel(c_ref, w_ref, b_ref, o_ref):
    sc = _silu(c_ref[...]).astype(BF16)
    o_ref[...] = _dot(sc, w_ref[...].astype(BF16)) + b_ref[...]


def _adaln(c, w, b):
    bsz, d = c.shape
    n = w.shape[1]
    tn = n // 8
    return pl.pallas_call(
        _adaln_kernel,
        grid=(n // tn,),
        in_specs=[pl.BlockSpec((bsz, d), lambda j: (0, 0)),
                  pl.BlockSpec((d, tn), lambda j: (0, j)),
                  pl.BlockSpec((1, tn), lambda j: (0, j))],
        out_specs=pl.BlockSpec((bsz, tn), lambda j: (0, j)),
        out_shape=jax.ShapeDtypeStruct((bsz, n), F32),
        compiler_params=_params("arbitrary"),
        name="adaln",
    )(c, w, b.reshape(1, n))


def _ffn_kernel(x_ref, mod_ref, w1_ref, w3_ref, w2_ref, g_ref, b_ref, o_ref, u_ref, acc_ref,
                *, mod_base, n_chunks, n_row_chunks):
    shift = mod_ref[0, mod_base:mod_base + 1, :]
    scale = mod_ref[0, mod_base + 1:mod_base + 2, :]
    gate = mod_ref[0, mod_base + 2:mod_base + 3, :]

    def prologue(r, carry):
        rows = pl.ds(pl.multiple_of(r * ROW_CHUNK, ROW_CHUNK), ROW_CHUNK)
        u_ref[rows, :] = (x_ref[0, rows, :] * (1.0 + scale) + shift).astype(BF16)
        return carry

    lax.fori_loop(0, n_row_chunks, prologue, 0)

    def chunk(f, carry):
        u = u_ref[...]
        a = _dot(u, w1_ref[f])
        b = _dot(u, w3_ref[f])
        hid = (_silu(a) * b).astype(BF16)
        contrib = _dot(hid, w2_ref[f])

        @pl.when(f == 0)
        def _():
            acc_ref[...] = contrib

        @pl.when(f > 0)
        def _():
            acc_ref[...] += contrib

        return carry

    lax.fori_loop(0, n_chunks, chunk, 0)

    def epilogue(r, carry):
        rows = pl.ds(pl.multiple_of(r * ROW_CHUNK, ROW_CHUNK), ROW_CHUNK)
        y = DEEPNORM_ALPHA * x_ref[0, rows, :] + (MACARON_WEIGHT * gate) * acc_ref[rows, :]
        o_ref[0, rows, :] = _layer_norm(y, g_ref[...], b_ref[...])
        return carry

    lax.fori_loop(0, n_row_chunks, epilogue, 0)


def _ffn_block(x, mod, w1, w3, w2, ln_g, ln_b, *, mod_base):
    bsz, seq, d = x.shape
    dff = w1.shape[1]
    n_chunks = dff // FFN_CHUNK
    tm = min(FFN_TOKENS, seq)
    w1c = w1.astype(BF16).reshape(d, n_chunks, FFN_CHUNK).transpose(1, 0, 2)
    w3c = w3.astype(BF16).reshape(d, n_chunks, FFN_CHUNK).transpose(1, 0, 2)
    w2c = w2.astype(BF16).reshape(n_chunks, FFN_CHUNK, d)
    kern = functools.partial(_ffn_kernel, mod_base=mod_base, n_chunks=n_chunks,
                             n_row_chunks=tm // ROW_CHUNK)
    const3 = lambda b, i: (0, 0, 0)
    return pl.pallas_call(
        kern,
        grid=(bsz, seq // tm),
        in_specs=[pl.BlockSpec((1, tm, d), lambda b, i: (b, i, 0)),
                  pl.BlockSpec((1, N_MOD, d), lambda b, i: (b, 0, 0)),
                  pl.BlockSpec((n_chunks, d, FFN_CHUNK), const3, pipeline_mode=pl.Buffered(1)),
                  pl.BlockSpec((n_chunks, d, FFN_CHUNK), const3, pipeline_mode=pl.Buffered(1)),
                  pl.BlockSpec((n_chunks, FFN_CHUNK, d), const3, pipeline_mode=pl.Buffered(1)),
                  pl.BlockSpec((1, d), lambda b, i: (0, 0)),
                  pl.BlockSpec((1, d), lambda b, i: (0, 0))],
        out_specs=pl.BlockSpec((1, tm, d), lambda b, i: (b, i, 0)),
        out_shape=jax.ShapeDtypeStruct((bsz, seq, d), F32),
        scratch_shapes=[pltpu.VMEM((tm, d), BF16), pltpu.VMEM((tm, d), F32)],
        compiler_params=_params("arbitrary", "arbitrary"),
        name="ffn",
    )(x, mod, w1c, w3c, w2c, ln_g.reshape(1, d), ln_b.reshape(1, d))


_ROW_Q = 0
_ROW_QI = _ROW_Q + A_HEADS * A_HEAD_DIM
_ROW_K = _ROW_QI + IDX_HEADS * IDX_DIM
_ROW_KI = _ROW_K + A_HEAD_DIM
_ROW_V = _ROW_KI + IDX_DIM
_ROW_W = _ROW_V + A_HEAD_DIM
_ROWS_A = _ROW_W + IDX_HEADS
_ROWS_A_PADDED = -(-_ROWS_A // 16) * 16
_COL_QKV = 0
_COL_Z = 2 * B_HEADS * B_KEY_DIM + B_HEADS * B_VAL_DIM
_COL_AB = _COL_Z + B_HEADS * B_VAL_DIM
_COLS_B = _COL_AB + LANES


def _proj_kernel(x_ref, mod_ref, pos_ref, invf_ref, wat_ref, wb_ref,
                 qt_ref, qit_ref, wt_ref, k_ref, ki_ref, vt_ref, qkv_ref, z_ref, ab_ref, u_ref,
                 *, n_row_chunks):
    shift = mod_ref[0, 3:4, :]
    scale = mod_ref[0, 4:5, :]

    def prologue(r, carry):
        rows = pl.ds(pl.multiple_of(r * ROW_CHUNK, ROW_CHUNK), ROW_CHUNK)
        u_ref[rows, :] = (x_ref[0, rows, :] * (1.0 + scale) + shift).astype(BF16)
        return carry

    lax.fori_loop(0, n_row_chunks, prologue, 0)
    u = u_ref[...]

    qkv_ref[0] = _dot(u, wb_ref[:, _COL_QKV:_COL_Z])
    z_ref[0] = _dot(u, wb_ref[:, _COL_Z:_COL_AB])
    ab_ref[0] = _dot(u, wb_ref[:, _COL_AB:_COLS_B])

    ang = invf_ref[...] * pos_ref[0]
    cos = jnp.cos(ang)
    sin = jnp.sin(ang)
    half = A_HEAD_DIM // 2

    def roped(row0):
        t = _dot_nt(wat_ref[row0:row0 + A_HEAD_DIM, :], u)
        t1 = t[:half]
        t2 = t[half:]
        return t1 * cos - t2 * sin, t2 * cos + t1 * sin

    def store_heads(dst_ref, row_base, scale):
        for h in range(A_HEADS):
            o1, o2 = roped(row_base + h * A_HEAD_DIM)
            o1 = (o1 * scale).astype(BF16)
            o2 = (o2 * scale).astype(BF16)
            lanes = slice(h * Q_BLOCK, (h + 1) * Q_BLOCK)
            for j in range(dst_ref.shape[1]):
                toks = slice(j * Q_BLOCK, (j + 1) * Q_BLOCK)
                dst_ref[0, j, :half, lanes] = o1[:, toks]
                dst_ref[0, j, half:, lanes] = o2[:, toks]

    store_heads(qt_ref, _ROW_Q, A_HEAD_DIM ** -0.5)
    store_heads(qit_ref, _ROW_QI, 1.0)

    k1, k2 = roped(_ROW_K)
    ki1, ki2 = roped(_ROW_KI)
    kk = jnp.concatenate([k1, k2, ki1, ki2], axis=0).T
    k_ref[0] = kk[:, :A_HEAD_DIM].astype(BF16)
    ki_ref[0] = kk[:, A_HEAD_DIM:].astype(BF16)

    vt = _dot_nt(wat_ref[_ROW_V:_ROW_V + A_HEAD_DIM, :], u).astype(BF16)
    for j in range(vt_ref.shape[1]):
        vt_ref[0, j] = vt[:, j * KEY_CHUNK:(j + 1) * KEY_CHUNK]

    wt = _dot_nt(wat_ref[_ROW_W:_ROWS_A_PADDED, :], u)
    wt_ref[0] = wt[:IDX_HEADS] * (IDX_HEADS ** -0.5 * IDX_DIM ** -0.5)


def _proj(x, mod, pos_f, w_in):
    bsz, seq, d = x.shape
    tm = min(PROJ_TOKENS, seq)
    splits = (A_HEADS * A_HEAD_DIM, A_HEAD_DIM, A_HEAD_DIM, IDX_HEADS * IDX_DIM, IDX_DIM, IDX_HEADS,
              B_HEADS * B_KEY_DIM, B_HEADS * B_KEY_DIM, B_HEADS * B_VAL_DIM, B_HEADS * B_VAL_DIM,
              B_HEADS, B_HEADS)
    offs = [0]
    for s in splits:
        offs.append(offs[-1] + s)
    cols = [w_in[:, offs[i]:offs[i + 1]] for i in range(len(splits))]
    a_q, a_k, a_v, i_q, i_k, i_w, b_q, b_k, b_v, b_z, b_a, b_b = cols
    wat = jnp.concatenate([a_q, i_q, a_k, i_k, a_v, i_w], axis=1).T.astype(BF16)
    wat = jnp.pad(wat, ((0, _ROWS_A_PADDED - _ROWS_A), (0, 0)))
    wb = jnp.concatenate([b_q, b_k, b_v, b_z, b_a, b_b], axis=1).astype(BF16)
    wb = jnp.pad(wb, ((0, 0), (0, _COLS_B - wb.shape[1])))
    half = A_HEAD_DIM // 2
    inv_freq = (ROPE_THETA ** (-jnp.arange(0, A_HEAD_DIM, 2, dtype=F32) / A_HEAD_DIM)).reshape(half, 1)

    n_kc = tm // KEY_CHUNK
    qkv_w = _COL_Z
    z_w = _COL_AB - _COL_Z
    tok = lambda b, i: (b, i, 0)
    tok_t = lambda b, i: (b, 0, i)
    const2 = lambda b, i: (0, 0)
    n_qb = tm // Q_BLOCK
    q_spec = pl.BlockSpec((1, n_qb, A_HEAD_DIM, A_HEADS * Q_BLOCK), lambda b, i: (b, i, 0, 0))
    out_shapes = (
        jax.ShapeDtypeStruct((bsz, seq // Q_BLOCK, A_HEAD_DIM, A_HEADS * Q_BLOCK), BF16),
        jax.ShapeDtypeStruct((bsz, seq // Q_BLOCK, IDX_DIM, IDX_HEADS * Q_BLOCK), BF16),
        jax.ShapeDtypeStruct((bsz, IDX_HEADS, seq), F32),
        jax.ShapeDtypeStruct((bsz, seq, A_HEAD_DIM), BF16),
        jax.ShapeDtypeStruct((bsz, seq, IDX_DIM), BF16),
        jax.ShapeDtypeStruct((bsz, seq // KEY_CHUNK, A_HEAD_DIM, KEY_CHUNK), BF16),
        jax.ShapeDtypeStruct((bsz, seq, qkv_w), F32),
        jax.ShapeDtypeStruct((bsz, seq, z_w), F32),
        jax.ShapeDtypeStruct((bsz, seq, LANES), F32),
    )
    out_specs = (
        q_spec,
        q_spec,
        pl.BlockSpec((1, IDX_HEADS, tm), tok_t),
        pl.BlockSpec((1, tm, A_HEAD_DIM), tok),
        pl.BlockSpec((1, tm, IDX_DIM), tok),
        pl.BlockSpec((1, n_kc, A_HEAD_DIM, KEY_CHUNK), lambda b, i: (b, i, 0, 0)),
        pl.BlockSpec((1, tm, qkv_w), tok),
        pl.BlockSpec((1, tm, z_w), tok),
        pl.BlockSpec((1, tm, LANES), tok),
    )
    kern = functools.partial(_proj_kernel, n_row_chunks=tm // ROW_CHUNK)
    return pl.pallas_call(
        kern,
        grid=(bsz, seq // tm),
        in_specs=[pl.BlockSpec((1, tm, d), tok),
                  pl.BlockSpec((1, N_MOD, d), lambda b, i: (b, 0, 0)),
                  pl.BlockSpec((1, 1, tm), tok_t),
                  pl.BlockSpec((half, 1), const2),
                  pl.BlockSpec((_ROWS_A_PADDED, d), const2),
                  pl.BlockSpec((d, _COLS_B), const2)],
        out_specs=out_specs,
        out_shape=out_shapes,
        scratch_shapes=[pltpu.VMEM((tm, d), BF16)],
        compiler_params=_params("arbitrary", "arbitrary"),
        name="proj",
    )(x, mod, pos_f, inv_freq, wat, wb)


_KEY_LO = -2139095041
_KEY_HI = 2139095041


def _key_to_float(key):
    bits = jnp.where(key < 0, key ^ 0x7FFFFFFF, key)
    return lax.bitcast_convert_type(bits, F32)


def _fold_rows(t):
    return t.reshape(t.shape[0] // SUBLANES, SUBLANES, t.shape[1])


def _dsa_kernel(qt_ref, qit_ref, wt_ref, k_ref, ki_ref, vt_ref, o_ref, s_ref, lg_ref, ot_ref,
                *, n_sel, search_steps):
    blk = pl.program_id(1)
    kc = KEY_CHUNK
    q_base = blk * Q_BLOCK
    n_keys = q_base + Q_BLOCK
    n_kc = (n_keys + kc - 1) // kc
    row = lax.broadcasted_iota(jnp.int32, (kc, Q_BLOCK), 0)
    col = lax.broadcasted_iota(jnp.int32, (kc, Q_BLOCK), 1)
    wt = wt_ref[0]
    heads = range(A_HEADS)
    lanes = [slice(h * Q_BLOCK, (h + 1) * Q_BLOCK) for h in heads]

    def chunk_rows(c):
        return pl.ds(pl.multiple_of(c * kc, kc), kc)

    def score_chunk(c, carry):
        x = _dot(ki_ref[0, chunk_rows(c), :], qit_ref[0, 0])
        acc = wt[0:1, :] * jnp.maximum(x[:, lanes[0]], 0.0)
        for h in range(1, IDX_HEADS):
            acc = acc + wt[h:h + 1, :] * jnp.maximum(x[:, lanes[h]], 0.0)
        causal = row + (c * kc - q_base) <= col
        s_ref[c] = jnp.where(causal, acc, NEG_INF)
        return carry

    lax.fori_loop(0, n_kc, score_chunk, 0)

    def count(pred):
        def body(c, acc):
            return acc + jnp.sum(_fold_rows(jnp.where(pred(s_ref[c]), 1.0, 0.0)), axis=0)

        acc = lax.fori_loop(0, n_kc, body, jnp.zeros((SUBLANES, Q_BLOCK), F32))
        return jnp.sum(acc, axis=0, keepdims=True)

    def rewrite_scores(fn):
        def body(c, carry):
            s_ref[c] = fn(s_ref[c])
            return carry
        lax.fori_loop(0, n_kc, body, 0)

    @pl.when(n_keys <= n_sel)
    def _():
        rewrite_scores(lambda s: jnp.where(s > NEG_INF, 0.0, NEG_INF))

    @pl.when(n_keys > n_sel)
    def _():
        def bisect(i, carry):
            lo, hi, cnt_lo = carry
            mid = (lo & hi) + ((lo ^ hi) >> 1)
            thr = _key_to_float(mid)
            cnt = count(lambda s: s >= thr)
            ok = cnt >= n_sel
            return jnp.where(ok, mid, lo), jnp.where(ok, hi, mid), jnp.where(ok, cnt, cnt_lo)

        lo0 = jnp.full((1, Q_BLOCK), _KEY_LO, jnp.int32)
        hi0 = jnp.full((1, Q_BLOCK), _KEY_HI, jnp.int32)
        cnt0 = jnp.full((1, Q_BLOCK), float(n_sel), F32)
        lo, _, cnt_lo = lax.fori_loop(0, search_steps, bisect, (lo0, hi0, cnt0))
        thr = _key_to_float(lo)
        has_ties = jnp.max(cnt_lo) > n_sel

        @pl.when(jnp.logical_not(has_ties))
        def _():
            rewrite_scores(lambda s: jnp.where(s >= thr, 0.0, NEG_INF))

        @pl.when(has_ties)
        def _():
            quota = n_sel - count(lambda s: s > thr)
            lower = (lax.broadcasted_iota(jnp.int32, (kc, kc), 1)
                     < lax.broadcasted_iota(jnp.int32, (kc, kc), 0))
            lower = jnp.where(lower, 1.0, 0.0).astype(BF16)

            def body(c, seen):
                s = s_ref[c]
                eq = jnp.where(s == thr, 1.0, 0.0)
                rank = _dot(lower, eq.astype(BF16)) + seen
                take = jnp.where(s > thr, 1.0, jnp.where(rank < quota, eq, 0.0))
                s_ref[c] = jnp.where(take > 0.5, 0.0, NEG_INF)
                return seen + jnp.sum(eq, axis=0, keepdims=True)

            lax.fori_loop(0, n_kc, body, jnp.zeros((1, Q_BLOCK), F32))

    def logits_chunk(c, m):
        bias = s_ref[c]
        lg = _dot(k_ref[0, chunk_rows(c), :], qt_ref[0, 0]) + jnp.concatenate([bias] * A_HEADS, axis=1)
        lg_ref[c] = lg
        return jnp.maximum(m, jnp.max(_fold_rows(lg), axis=0))

    m = lax.fori_loop(0, n_kc, logits_chunk, jnp.full((SUBLANES, A_HEADS * Q_BLOCK), NEG_INF, F32))
    m = jnp.max(m, axis=0, keepdims=True)

    ot_ref[...] = jnp.zeros(ot_ref.shape, F32)

    def pv_chunk(c, l):
        p = jnp.exp(lg_ref[c] - m)
        ot_ref[...] += _dot(vt_ref[0, c], p.astype(BF16))
        return l + jnp.sum(_fold_rows(p), axis=0)

    l = lax.fori_loop(0, n_kc, pv_chunk, jnp.zeros((SUBLANES, A_HEADS * Q_BLOCK), F32))
    o = ot_ref[...] / jnp.sum(l, axis=0, keepdims=True)
    o = jnp.concatenate([o[:, lanes[h]] for h in heads], axis=0)
    o_ref[0] = o.T.astype(BF16)


def _dsa(qt, qit, wt, k, ki, vt):
    bsz, n_blk, _, hq = qt.shape
    seq = n_blk * Q_BLOCK
    n_sel = min(TOPK_MAX, seq // 4)
    n_kc = seq // KEY_CHUNK
    hd = A_HEADS * A_HEAD_DIM
    kern = functools.partial(_dsa_kernel, n_sel=n_sel, search_steps=32)
    return pl.pallas_call(
        kern,
        grid=(bsz, n_blk),
        in_specs=[pl.BlockSpec((1, 1, A_HEAD_DIM, hq), lambda b, i: (b, i, 0, 0)),
                  pl.BlockSpec((1, 1, IDX_DIM, hq), lambda b, i: (b, i, 0, 0)),
                  pl.BlockSpec((1, IDX_HEADS, Q_BLOCK), lambda b, i: (b, 0, i)),
                  pl.BlockSpec((1, seq, A_HEAD_DIM), lambda b, i: (b, 0, 0)),
                  pl.BlockSpec((1, seq, IDX_DIM), lambda b, i: (b, 0, 0)),
                  pl.BlockSpec((1, n_kc, A_HEAD_DIM, KEY_CHUNK), lambda b, i: (b, 0, 0, 0))],
        out_specs=pl.BlockSpec((1, Q_BLOCK, hd), lambda b, i: (b, i, 0)),
        out_shape=jax.ShapeDtypeStruct((bsz, seq, hd), BF16),
        scratch_shapes=[pltpu.VMEM((n_kc, KEY_CHUNK, Q_BLOCK), F32),
                        pltpu.VMEM((n_kc, KEY_CHUNK, hq), F32),
                        pltpu.VMEM((A_HEAD_DIM, hq), F32)],
        compiler_params=_params("arbitrary", "arbitrary"),
        name="dsa",
    )(qt, qit, wt, k, ki, vt)


def _softplus(x):
    return jnp.maximum(x, 0.0) + jnp.log(1.0 + jnp.exp(-jnp.abs(x)))


def _split3(x):
    hi = x.astype(BF16)
    r = x - hi.astype(F32)
    mid = r.astype(BF16)
    lo = (r - mid.astype(F32)).astype(BF16)
    return hi, mid, lo


def _gdn_kernel(q_ref, k_ref, v_ref, z_ref, ab_ref, cwq_ref, cwk_ref, cwv_ref, alog_ref, dtb_ref, ng_ref,
                o_ref, st_ref, *, n_steps, heads):
    c = GDN_CHUNK
    dk = B_KEY_DIM
    halo = SUBLANES

    def conv_head(src_ref, w_ref, r0, prev0, live, h):
        cols = slice(h * dk, (h + 1) * dk)
        cur = src_ref[0, pl.ds(r0, c), cols]
        prev = src_ref[0, pl.ds(prev0, halo), cols] * live
        xw = jnp.concatenate([prev, cur], axis=0)
        y = cur * w_ref[CONV_WIDTH - 1:CONV_WIDTH, cols]
        for j in range(CONV_WIDTH - 1):
            sh = CONV_WIDTH - 1 - j
            y = y + xw[halo - sh:halo - sh + c] * w_ref[j:j + 1, cols]
        return _silu(y)

    def l2n(t):
        return t * lax.rsqrt(jnp.sum(t * t, axis=-1, keepdims=True) + RMS_EPS)

    st_ref[...] = jnp.zeros(st_ref.shape, F32)
    ri = lax.broadcasted_iota(jnp.int32, (c, c), 0)
    ci = lax.broadcasted_iota(jnp.int32, (c, c), 1)
    strict = ri > ci
    eye = jnp.where(ri == ci, 1.0, 0.0)
    incl_ones = jnp.where(ri >= ci, 1.0, 0.0).astype(BF16)
    inv_shift = GDN_INV_BLOCK.bit_length() - 1
    diag_mask = jnp.where((ri >> inv_shift) == (ci >> inv_shift), 1.0, 0.0)
    level_masks = []
    for ls in range(inv_shift, c.bit_length() - 1):
        lower_left = jnp.where((ri >> ls) == (ci >> ls) + 1, 1.0, 0.0) * jnp.where(((ci >> ls) & 1) == 0, 1.0, 0.0)
        level_masks.append(lower_left)

    def step(s, carry):
        r0 = pl.multiple_of(s * c, c)
        prev0 = pl.multiple_of(jnp.maximum(r0 - halo, 0), halo)
        live = jnp.where(s > 0, 1.0, 0.0)
        rows = pl.ds(r0, c)
        ab = ab_ref[0, rows, :]
        log_a = -jnp.exp(alog_ref[...]) * _softplus(ab + dtb_ref[...])
        beta_all = 1.0 / (1.0 + jnp.exp(-ab))
        la_hi, la_mid, la_lo = _split3(log_a)
        gam_all = _dot(incl_ones, la_hi) + _dot(incl_ones, la_mid) + _dot(incl_ones, la_lo)

        hs = range(heads)
        gam = [gam_all[:, h:h + 1] for h in hs]
        beta = [beta_all[:, B_HEADS + h:B_HEADS + h + 1] for h in hs]
        cols = [slice(h * dk, (h + 1) * dk) for h in hs]
        kh = [l2n(conv_head(k_ref, cwk_ref, r0, prev0, live, h)) for h in hs]
        qh = [l2n(conv_head(q_ref, cwq_ref, r0, prev0, live, h)) * (dk ** -0.5) for h in hs]
        vh = [conv_head(v_ref, cwv_ref, r0, prev0, live, h) for h in hs]
        kb = [kh[h].astype(BF16) for h in hs]
        kk = [_dot_nt(kb[h], kb[h]) for h in hs]
        qk = [_dot_nt(qh[h].astype(BF16), kb[h]) for h in hs]

        dec = []
        for h in hs:
            gcol = jnp.broadcast_to(gam[h], (c, c))
            diff = gcol - gcol.T
            dec.append(jnp.where(strict, jnp.exp(jnp.where(strict, diff, 0.0)), 0.0))
        nmat = [beta[h] * kk[h] * dec[h] for h in hs]
        a_qk = [(qk[h] * (dec[h] + eye)).astype(BF16) for h in hs]

        mp = [-(nmat[h] * diag_mask) for h in hs]
        x = [eye + mp[h] for h in hs]
        for _ in range(GDN_INV_BLOCK.bit_length() - 2):
            mpb = [mp[h].astype(BF16) for h in hs]
            mp = [_dot(mpb[h], mpb[h]) for h in hs]
            x = [x[h] + _dot(x[h].astype(BF16), mp[h].astype(BF16)) for h in hs]
        for lm in level_masks:
            xb = [x[h].astype(BF16) for h in hs]
            cx = [_dot((nmat[h] * lm).astype(BF16), xb[h]).astype(BF16) for h in hs]
            x = [x[h] - _dot(xb[h], cx[h]) for h in hs]
        xb = [x[h].astype(BF16) for h in hs]

        g_cum = [jnp.exp(gam[h]) for h in hs]
        w = [_dot(xb[h], ((beta[h] * g_cum[h]) * kh[h]).astype(BF16)).astype(BF16) for h in hs]
        u = [_dot(xb[h], (beta[h] * vh[h]).astype(BF16)) for h in hs]
        g_last = [gam[h][c - 1:c, :] for h in hs]
        q_dec = [(qh[h] * g_cum[h]).astype(BF16) for h in hs]
        k_dec = [(kh[h] * jnp.exp(g_last[h] - gam[h])).astype(BF16) for h in hs]

        state = [st_ref[h] for h in hs]
        sb = [state[h].astype(BF16) for h in hs]
        db = [(u[h] - _dot(w[h], sb[h])).astype(BF16) for h in hs]
        out = [_dot(q_dec[h], sb[h]) + _dot(a_qk[h], db[h]) for h in hs]
        for h in hs:
            st_ref[h] = jnp.exp(g_last[h]) * state[h] + _dot_tn(k_dec[h], db[h])
        for h in hs:
            normed = (out[h] * lax.rsqrt(jnp.mean(out[h] * out[h], axis=-1, keepdims=True) + RMS_EPS)
                      * ng_ref[...])
            o_ref[0, rows, cols[h]] = (normed * _silu(z_ref[0, rows, cols[h]])).astype(BF16)
        return carry

    lax.fori_loop(0, n_steps, step, 0)


def _gdn(qkv, z, ab, conv_w, a_log, dt_bias, norm_g):
    bsz, seq, _ = qkv.shape
    heads = B_HEADS
    gw = heads * B_KEY_DIM
    pad_lanes = lambda v: jnp.pad(v.reshape(1, -1), ((0, 0), (0, LANES - v.size)))
    kern = functools.partial(_gdn_kernel, n_steps=seq // GDN_CHUNK, heads=heads)
    col_block = lambda off: (lambda b: (b, 0, off))
    w_block = lambda off: (lambda b: (0, off))
    const2 = lambda b: (0, 0)
    return pl.pallas_call(
        kern,
        grid=(bsz,),
        in_specs=[pl.BlockSpec((1, seq, gw), col_block(0)),
                  pl.BlockSpec((1, seq, gw), col_block(1)),
                  pl.BlockSpec((1, seq, gw), col_block(2)),
                  pl.BlockSpec((1, seq, gw), col_block(0)),
                  pl.BlockSpec((1, seq, LANES), lambda b: (b, 0, 0)),
                  pl.BlockSpec((CONV_WIDTH, gw), w_block(0)),
                  pl.BlockSpec((CONV_WIDTH, gw), w_block(1)),
                  pl.BlockSpec((CONV_WIDTH, gw), w_block(2)),
                  pl.BlockSpec((1, LANES), const2),
                  pl.BlockSpec((1, LANES), const2),
                  pl.BlockSpec((1, B_VAL_DIM), const2)],
        out_specs=pl.BlockSpec((1, seq, gw), col_block(0)),
        out_shape=jax.ShapeDtypeStruct((bsz, seq, B_HEADS * B_VAL_DIM), BF16),
        scratch_shapes=[pltpu.VMEM((heads, B_KEY_DIM, B_VAL_DIM), F32)],
        compiler_params=_params("arbitrary"),
        name="gdn",
    )(qkv, qkv, qkv, z, ab, conv_w, conv_w, conv_w, pad_lanes(a_log), pad_lanes(dt_bias),
      norm_g.reshape(1, B_VAL_DIM))


def _outproj_kernel(x_ref, mod_ref, attn_ref, dn_ref, wo_ref, g_ref, b_ref, o_ref, *, n_row_chunks):
    gate = mod_ref[0, 5:6, :]
    n_attn = attn_ref.shape[2]

    def body(r, carry):
        rows = pl.ds(pl.multiple_of(r * ROW_CHUNK, ROW_CHUNK), ROW_CHUNK)
        h = _dot(attn_ref[0, rows, :], wo_ref[:n_attn, :]) + _dot(dn_ref[0, rows, :], wo_ref[n_attn:, :])
        y = DEEPNORM_ALPHA * x_ref[0, rows, :] + gate * h
        o_ref[0, rows, :] = _layer_norm(y, g_ref[...], b_ref[...])
        return carry

    lax.fori_loop(0, n_row_chunks, body, 0)


def _outproj(x, mod, attn, dn, w_out, ln_g, ln_b):
    bsz, seq, d = x.shape
    tm = min(OUT_TOKENS, seq)
    tok = lambda b, i: (b, i, 0)
    const2 = lambda b, i: (0, 0)
    kern = functools.partial(_outproj_kernel, n_row_chunks=tm // ROW_CHUNK)
    return pl.pallas_call(
        kern,
        grid=(bsz, seq // tm),
        in_specs=[pl.BlockSpec((1, tm, d), tok),
                  pl.BlockSpec((1, N_MOD, d), lambda b, i: (b, 0, 0)),
                  pl.BlockSpec((1, tm, attn.shape[2]), tok),
                  pl.BlockSpec((1, tm, dn.shape[2]), tok),
                  pl.BlockSpec(w_out.shape, const2),
                  pl.BlockSpec((1, d), const2),
                  pl.BlockSpec((1, d), const2)],
        out_specs=pl.BlockSpec((1, tm, d), tok),
        out_shape=jax.ShapeDtypeStruct((bsz, seq, d), F32),
        compiler_params=_params("arbitrary", "arbitrary"),
        name="outproj",
    )(x, mod, attn, dn, w_out.astype(BF16), ln_g.reshape(1, d), ln_b.reshape(1, d))


def kernel(x, c, positions, w_ada, b_ada, ffn1_w1, ffn1_w3, ffn1_w2, ln1_g, ln1_b, w_in, conv_w, a_log, dt_bias, dn_norm_g, w_out, ln2_g, ln2_b, ffn2_w1, ffn2_w3, ffn2_w2, ln3_g, ln3_b):
    bsz, seq, d = x.shape
    pos_f = positions.astype(F32).reshape(bsz, 1, seq)
    for layer in range(w_ada.shape[0]):
        mod = _adaln(c, w_ada[layer], b_ada[layer]).reshape(bsz, N_MOD, d)
        x = _ffn_block(x, mod, ffn1_w1[layer], ffn1_w3[layer], ffn1_w2[layer], ln1_g[layer], ln1_b[layer],
                       mod_base=0)
        qt, qit, wt, k, ki, vt, qkv, z, ab = _proj(x, mod, pos_f, w_in[layer])
        attn = _dsa(qt, qit, wt, k, ki, vt)
        dn = _gdn(qkv, z, ab, conv_w[layer], a_log[layer], dt_bias[layer], dn_norm_g[layer])
        x = _outproj(x, mod, attn, dn, w_out[layer], ln2_g[layer], ln2_b[layer])
        x = _ffn_block(x, mod, ffn2_w1[layer], ffn2_w3[layer], ffn2_w2[layer], ln3_g[layer], ln3_b[layer],
                       mod_base=6)
    return x
```

```python
import functools

import jax
import jax.numpy as jnp
from jax import lax
from jax.experimental import pallas as pl
from jax.experimental.pallas import tpu as pltpu

F32 = jnp.float32
BF16 = jnp.bfloat16

A_HEADS = 8
A_HEAD_DIM = 64
IDX_HEADS = 8
IDX_DIM = 64
TOPK_MAX = 256
ROPE_THETA = 10000.0
B_HEADS = 4
B_KEY_DIM = 128
B_VAL_DIM = 128
CONV_WIDTH = 4
MACARON_WEIGHT = 0.5
DEPTH = 1
DEEPNORM_ALPHA = (2.0 * DEPTH) ** 0.25
LN_EPS = 1e-5
RMS_EPS = 1e-6
N_MOD = 9

LANES = 128
SUBLANES = 8
VMEM_LIMIT_BYTES = 56 * 1024 * 1024

FFN_TOKENS = 1024
FFN_CHUNK = 256
FFN_OUT_ROWS = 512
PROJ_TOKENS = 512
OUT_TOKENS = 1024
ROW_CHUNK = 256
Q_BLOCK = 256
KEY_CHUNK = 256
GDN_CHUNK = 256
GDN_HEADS_PER_STEP = 2
GDN_INV_BLOCK = 16

COUNT_ROWS = 32

NEG_INF = float("-inf")


def _silu(x):
    return x * (1.0 / (1.0 + jnp.exp(-x)))


def _dot(a, b):
    return jnp.dot(a, b, preferred_element_type=F32)


def _dot_nt(a, b):
    return lax.dot_general(a, b, (((1,), (1,)), ((), ())), preferred_element_type=F32)


def _dot_tn(a, b):
    return lax.dot_general(a, b, (((0,), (0,)), ((), ())), preferred_element_type=F32)


def _layer_norm(y, g, b):
    mu = jnp.mean(y, axis=-1, keepdims=True)
    d = y - mu
    var = jnp.mean(d * d, axis=-1, keepdims=True)
    return d * lax.rsqrt(var + LN_EPS) * g + b


def _params(*sem):
    return pltpu.CompilerParams(dimension_semantics=sem, vmem_limit_bytes=VMEM_LIMIT_BYTES)


def _adaln_kernel(c_ref, w_ref, b_ref, o_ref):
    sc = _silu(c_ref[...]).astype(BF16)
    o_ref[...] = _dot(sc, w_ref[...].astype(BF16)) + b_ref[...]


def _adaln(c, w, b):
    bsz, d = c.shape
    n = w.shape[1]
    tn = n // 8
    return pl.pallas_call(
        _adaln_kernel,
        grid=(n // tn,),
        in_specs=[pl.BlockSpec((bsz, d), lambda j: (0, 0)),
                  pl.BlockSpec((d, tn), lambda j: (0, j)),
                  pl.BlockSpec((1, tn), lambda j: (0, j))],
        out_specs=pl.BlockSpec((bsz, tn), lambda j: (0, j)),
        out_shape=jax.ShapeDtypeStruct((bsz, n), F32),
        compiler_params=_params("arbitrary"),
        name="adaln",
    )(c, w, b.reshape(1, n))


def _ffn_kernel(x_ref, mod_ref, w1_ref, w3_ref, w2_ref, g_ref, b_ref, o_ref, u_ref, hid_ref,
                *, mod_base, n_chunks, n_row_chunks, n_out_chunks):
    shift = mod_ref[0, mod_base:mod_base + 1, :]
    scale = mod_ref[0, mod_base + 1:mod_base + 2, :]
    gate = mod_ref[0, mod_base + 2:mod_base + 3, :]

    def prologue(r, carry):
        rows = pl.ds(pl.multiple_of(r * ROW_CHUNK, ROW_CHUNK), ROW_CHUNK)
        u_ref[rows, :] = (x_ref[0, rows, :] * (1.0 + scale) + shift).astype(BF16)
        return carry

    lax.fori_loop(0, n_row_chunks, prologue, 0)

    def chunk(f, carry):
        u = u_ref[...]
        a = _dot(u, w1_ref[f])
        b = _dot(u, w3_ref[f])
        hid_ref[f] = (_silu(a) * b).astype(BF16)
        return carry

    lax.fori_loop(0, n_chunks, chunk, 0)

    rows_per = x_ref.shape[1] // n_out_chunks
    for r in range(n_out_chunks):
        rows = slice(r * rows_per, (r + 1) * rows_per)
        hid = jnp.concatenate([hid_ref[f, rows, :] for f in range(n_chunks)], axis=1)
        h = _dot(hid, w2_ref[...])
        y =DEEPNORM_ALPHA * x_ref[0, rows, :] + (MACARON_WEIGHT * gate) * h
        o_ref[0, rows, :] = _layer_norm(y, g_ref[...], b_ref[...])


def _ffn_block(x, mod, w1, w3, w2, ln_g, ln_b, *, mod_base):
    bsz, seq, d = x.shape
    dff = w1.shape[1]
    n_chunks = dff // FFN_CHUNK
    tm = min(FFN_TOKENS, seq)
    w1c = w1.astype(BF16).reshape(d, n_chunks, FFN_CHUNK).transpose(1, 0, 2)
    w3c = w3.astype(BF16).reshape(d, n_chunks, FFN_CHUNK).transpose(1, 0, 2)
    w2c = w2.astype(BF16)
    kern = functools.partial(_ffn_kernel, mod_base=mod_base, n_chunks=n_chunks,
                             n_row_chunks=tm // ROW_CHUNK, n_out_chunks=max(1, tm // FFN_OUT_ROWS))
    const3 = lambda b, i: (0, 0, 0)
    return pl.pallas_call(
        kern,
        grid=(bsz, seq // tm),
        in_specs=[pl.BlockSpec((1, tm, d), lambda b, i: (b, i, 0)),
                  pl.BlockSpec((1, N_MOD, d), lambda b, i: (b, 0, 0)),
                  pl.BlockSpec((n_chunks, d, FFN_CHUNK), const3, pipeline_mode=pl.Buffered(1)),
                  pl.BlockSpec((n_chunks, d, FFN_CHUNK), const3, pipeline_mode=pl.Buffered(1)),
                  pl.BlockSpec((dff, d), lambda b, i: (0, 0), pipeline_mode=pl.Buffered(1)),
                  pl.BlockSpec((1, d), lambda b, i: (0, 0)),
                  pl.BlockSpec((1, d), lambda b, i: (0, 0))],
        out_specs=pl.BlockSpec((1, tm, d), lambda b, i: (b, i, 0)),
        out_shape=jax.ShapeDtypeStruct((bsz, seq, d), F32),
        scratch_shapes=[pltpu.VMEM((tm, d), BF16), pltpu.VMEM((n_chunks, tm, FFN_CHUNK), BF16)],
        compiler_params=_params("arbitrary", "arbitrary"),
        name="ffn",
    )(x, mod, w1c, w3c, w2c, ln_g.reshape(1, d), ln_b.reshape(1, d))


_ROW_Q = 0
_ROW_QI = _ROW_Q + A_HEADS * A_HEAD_DIM
_ROW_K = _ROW_QI + IDX_HEADS * IDX_DIM
_ROW_KI = _ROW_K + A_HEAD_DIM
_ROW_V = _ROW_KI + IDX_DIM
_ROW_W = _ROW_V + A_HEAD_DIM
_ROWS_A = _ROW_W + IDX_HEADS
_ROWS_A_PADDED = -(-_ROWS_A // 16) * 16
_COL_QKV = 0
_COL_Z = 2 * B_HEADS * B_KEY_DIM + B_HEADS * B_VAL_DIM
_COL_AB = _COL_Z + B_HEADS * B_VAL_DIM
_COLS_B = _COL_AB + LANES


def _proj_kernel(x_ref, mod_ref, pos_ref, invf_ref, wat_ref, wb_ref,
                 qt_ref, qit_ref, wt_ref, k_ref, ki_ref, vt_ref, qkv_ref, z_ref, ab_ref, u_ref, pat_ref,
                 *, n_row_chunks):
    shift = mod_ref[0, 3:4, :]
    scale = mod_ref[0, 4:5, :]

    def prologue(r, carry):
        rows = pl.ds(pl.multiple_of(r * ROW_CHUNK, ROW_CHUNK), ROW_CHUNK)
        u_ref[rows, :] = (x_ref[0, rows, :] * (1.0 + scale) + shift).astype(BF16)
        return carry

    lax.fori_loop(0, n_row_chunks, prologue, 0)
    u = u_ref[...]

    qkv_ref[0] = _dot(u, wb_ref[:, _COL_QKV:_COL_Z])
    z_ref[0] = _dot(u, wb_ref[:, _COL_Z:_COL_AB])
    ab_ref[0] = _dot(u, wb_ref[:, _COL_AB:_COLS_B])

    ang = invf_ref[...] * pos_ref[0]
    cos = jnp.cos(ang)
    sin = jnp.sin(ang)
    half = A_HEAD_DIM // 2

    pat_ref[...] = _dot_nt(wat_ref[...], u)

    def roped(row0):
        t1 = pat_ref[row0:row0 + half, :]
        t2 = pat_ref[row0 + half:row0 + A_HEAD_DIM, :]
        return t1 * cos - t2 * sin, t2 * cos + t1 * sin

    def store_heads(dst_ref, row_base, scale):
        for h in range(A_HEADS):
            o1, o2 = roped(row_base + h * A_HEAD_DIM)
            o1 = (o1 * scale).astype(BF16)
            o2 = (o2 * scale).astype(BF16)
            lanes = slice(h * Q_BLOCK, (h + 1) * Q_BLOCK)
            for j in range(dst_ref.shape[1]):
                toks = slice(j * Q_BLOCK, (j + 1) * Q_BLOCK)
                dst_ref[0, j, :half, lanes] = o1[:, toks]
                dst_ref[0, j, half:, lanes] = o2[:, toks]

    store_heads(qt_ref, _ROW_Q, A_HEAD_DIM ** -0.5)
    store_heads(qit_ref, _ROW_QI, 1.0)

    k1, k2 = roped(_ROW_K)
    ki1, ki2 = roped(_ROW_KI)
    kk = jnp.concatenate([k1, k2, ki1, ki2], axis=0).T
    k_ref[0] = kk[:, :A_HEAD_DIM].astype(BF16)
    ki_ref[0] = kk[:, A_HEAD_DIM:].astype(BF16)

    for j in range(vt_ref.shape[1]):
        vt_ref[0, j] = pat_ref[_ROW_V:_ROW_V + A_HEAD_DIM, j * KEY_CHUNK:(j + 1) * KEY_CHUNK].astype(BF16)

    wt_ref[0] = pat_ref[_ROW_W:_ROW_W + IDX_HEADS, :] * (IDX_HEADS ** -0.5 * IDX_DIM ** -0.5)


def _proj(x, mod, pos_f, w_in):
    bsz, seq, d = x.shape
    tm = min(PROJ_TOKENS, seq)
    splits = (A_HEADS * A_HEAD_DIM, A_HEAD_DIM, A_HEAD_DIM, IDX_HEADS * IDX_DIM, IDX_DIM, IDX_HEADS,
              B_HEADS * B_KEY_DIM, B_HEADS * B_KEY_DIM, B_HEADS * B_VAL_DIM, B_HEADS * B_VAL_DIM,
              B_HEADS, B_HEADS)
    offs = [0]
    for s in splits:
        offs.append(offs[-1] + s)
    cols = [w_in[:, offs[i]:offs[i + 1]] for i in range(len(splits))]
    a_q, a_k, a_v, i_q, i_k, i_w, b_q, b_k, b_v, b_z, b_a, b_b = cols
    wat = jnp.concatenate([a_q, i_q, a_k, i_k, a_v, i_w], axis=1).T.astype(BF16)
    wat = jnp.pad(wat, ((0, _ROWS_A_PADDED - _ROWS_A), (0, 0)))
    wb = jnp.concatenate([b_q, b_k, b_v, b_z, b_a, b_b], axis=1).astype(BF16)
    wb = jnp.pad(wb, ((0, 0), (0, _COLS_B - wb.shape[1])))
    half = A_HEAD_DIM // 2
    inv_freq = (ROPE_THETA ** (-jnp.arange(0, A_HEAD_DIM, 2, dtype=F32) / A_HEAD_DIM)).reshape(half, 1)

    n_kc = tm // KEY_CHUNK
    qkv_w = _COL_Z
    z_w = _COL_AB - _COL_Z
    tok = lambda b, i: (b, i, 0)
    tok_t = lambda b, i: (b, 0, i)
    const2 = lambda b, i: (0, 0)
    n_qb = tm // Q_BLOCK
    q_spec = pl.BlockSpec((1, n_qb, A_HEAD_DIM, A_HEADS * Q_BLOCK), lambda b, i: (b, i, 0, 0))
    out_shapes = (
        jax.ShapeDtypeStruct((bsz, seq // Q_BLOCK, A_HEAD_DIM, A_HEADS * Q_BLOCK), BF16),
        jax.ShapeDtypeStruct((bsz, seq // Q_BLOCK, IDX_DIM, IDX_HEADS * Q_BLOCK), BF16),
        jax.ShapeDtypeStruct((bsz, IDX_HEADS, seq), F32),
        jax.ShapeDtypeStruct((bsz, seq, A_HEAD_DIM), BF16),
        jax.ShapeDtypeStruct((bsz, seq, IDX_DIM), BF16),
        jax.ShapeDtypeStruct((bsz, seq // KEY_CHUNK, A_HEAD_DIM, KEY_CHUNK), BF16),
        jax.ShapeDtypeStruct((bsz, seq, qkv_w), F32),
        jax.ShapeDtypeStruct((bsz, seq, z_w), F32),
        jax.ShapeDtypeStruct((bsz, seq, LANES), F32),
    )
    out_specs = (
        q_spec,
        q_spec,
        pl.BlockSpec((1, IDX_HEADS, tm), tok_t),
        pl.BlockSpec((1, tm, A_HEAD_DIM), tok),
        pl.BlockSpec((1, tm, IDX_DIM), tok),
        pl.BlockSpec((1, n_kc, A_HEAD_DIM, KEY_CHUNK), lambda b, i: (b, i, 0, 0)),
        pl.BlockSpec((1, tm, qkv_w), tok),
        pl.BlockSpec((1, tm, z_w), tok),
        pl.BlockSpec((1, tm, LANES), tok),
    )
    kern = functools.partial(_proj_kernel, n_row_chunks=tm // ROW_CHUNK)
    return pl.pallas_call(
        kern,
        grid=(bsz, seq // tm),
        in_specs=[pl.BlockSpec((1, tm, d), tok),
                  pl.BlockSpec((1, N_MOD, d), lambda b, i: (b, 0, 0)),
                  pl.BlockSpec((1, 1, tm), tok_t),
                  pl.BlockSpec((half, 1), const2),
                  pl.BlockSpec((_ROWS_A_PADDED, d), const2),
                  pl.BlockSpec((d, _COLS_B), const2)],
        out_specs=out_specs,
        out_shape=out_shapes,
        scratch_shapes=[pltpu.VMEM((tm, d), BF16), pltpu.VMEM((_ROWS_A_PADDED, tm), F32)],
        compiler_params=_params("arbitrary", "arbitrary"),
        name="proj",
    )(x, mod, pos_f, inv_freq, wat, wb)


_KEY_LO = -2139095041
_KEY_HI = 2139095041


def _key_to_float(key):
    bits = jnp.where(key < 0, key ^ 0x7FFFFFFF, key)
    return lax.bitcast_convert_type(bits, F32)


def _fold_rows(t):
    return t.reshape(t.shape[0] // SUBLANES, SUBLANES, t.shape[1])


def _dsa_kernel(qt_ref, qit_ref, wt_ref, k_ref, ki_ref, vt_ref, o_ref, s_ref, lg_ref, ot_ref,
                *, n_sel, search_steps):
    blk = pl.program_id(1)
    kc = KEY_CHUNK
    q_base = blk * Q_BLOCK
    n_keys = q_base + Q_BLOCK
    n_kc = (n_keys + kc - 1) // kc
    row = lax.broadcasted_iota(jnp.int32, (kc, Q_BLOCK), 0)
    col = lax.broadcasted_iota(jnp.int32, (kc, Q_BLOCK), 1)
    wt = wt_ref[0]
    heads = range(A_HEADS)
    lanes = [slice(h * Q_BLOCK, (h + 1) * Q_BLOCK) for h in heads]

    def chunk_rows(c):
        return pl.ds(pl.multiple_of(c * kc, kc), kc)

    def score_chunk(c, carry):
        x = _dot(ki_ref[0, chunk_rows(c), :], qit_ref[0, 0])
        acc = wt[0:1, :] * jnp.maximum(x[:, lanes[0]], 0.0)
        for h in range(1, IDX_HEADS):
            acc = acc + wt[h:h + 1, :] * jnp.maximum(x[:, lanes[h]], 0.0)
        causal = row + (c * kc - q_base) <= col
        s_ref[c] = jnp.where(causal, acc, NEG_INF)
        return carry

    lax.fori_loop(0, n_kc, score_chunk, 0)

    def count(pred):
        def body(c, acc):
            ind = jnp.where(pred(s_ref[c]), 1.0, 0.0)
            return acc + jnp.sum(ind.reshape(kc // COUNT_ROWS, COUNT_ROWS, Q_BLOCK), axis=0)

        acc = lax.fori_loop(0, n_kc, body, jnp.zeros((COUNT_ROWS, Q_BLOCK), F32))
        return jnp.sum(acc, axis=0, keepdims=True)

    def rewrite_scores(fn):
        def body(c, carry):
            s_ref[c] = fn(s_ref[c])
            return carry
        lax.fori_loop(0, n_kc, body, 0)

    @pl.when(n_keys <= n_sel)
    def _():
        rewrite_scores(lambda s: jnp.where(s > NEG_INF, 0.0, NEG_INF))

    @pl.when(n_keys > n_sel)
    def _():
        def bisect(i, carry):
            lo, hi, cnt_lo = carry
            mid = (lo & hi) + ((lo ^ hi) >> 1)
            thr = _key_to_float(mid)
            cnt = count(lambda s: s >= thr)
            ok = cnt >= n_sel
            return jnp.where(ok, mid, lo), jnp.where(ok, hi, mid), jnp.where(ok, cnt, cnt_lo)

        lo0 = jnp.full((1, Q_BLOCK), _KEY_LO, jnp.int32)
        hi0 = jnp.full((1, Q_BLOCK), _KEY_HI, jnp.int32)
        cnt0 = jnp.full((1, Q_BLOCK), float(n_sel), F32)
        lo, _, cnt_lo = lax.fori_loop(0, search_steps, bisect, (lo0, hi0, cnt0))
        thr = _key_to_float(lo)
        has_ties = jnp.max(cnt_lo) > n_sel

        @pl.when(jnp.logical_not(has_ties))
        def _():
            rewrite_scores(lambda s: jnp.where(s >= thr, 0.0, NEG_INF))

        @pl.when(has_ties)
        def _():
            quota = n_sel - count(lambda s: s > thr)
            lower = (lax.broadcasted_iota(jnp.int32, (kc, kc), 1)
                     < lax.broadcasted_iota(jnp.int32, (kc, kc), 0))
            lower = jnp.where(lower, 1.0, 0.0).astype(BF16)

            def body(c, seen):
                s = s_ref[c]
                eq = jnp.where(s == thr, 1.0, 0.0)
                rank = _dot(lower, eq.astype(BF16)) + seen
                take = jnp.where(s > thr, 1.0, jnp.where(rank < quota, eq, 0.0))
                s_ref[c] = jnp.where(take > 0.5, 0.0, NEG_INF)
                return seen + jnp.sum(eq, axis=0, keepdims=True)

            lax.fori_loop(0, n_kc, body, jnp.zeros((1, Q_BLOCK), F32))

    def logits_chunk(c, m):
        bias = s_ref[c]
        lg = _dot(k_ref[0, chunk_rows(c), :], qt_ref[0, 0]) + jnp.concatenate([bias] * A_HEADS, axis=1)
        lg_ref[c] = lg
        return jnp.maximum(m, jnp.max(_fold_rows(lg), axis=0))

    m = lax.fori_loop(0, n_kc, logits_chunk, jnp.full((SUBLANES, A_HEADS * Q_BLOCK), NEG_INF, F32))
    m = jnp.max(m, axis=0, keepdims=True)

    ot_ref[...] = jnp.zeros(ot_ref.shape, F32)

    def pv_chunk(c, l):
        p = jnp.exp(lg_ref[c] - m)
        ot_ref[...] += _dot(vt_ref[0, c], p.astype(BF16))
        return l + jnp.sum(_fold_rows(p), axis=0)

    l = lax.fori_loop(0, n_kc, pv_chunk, jnp.zeros((SUBLANES, A_HEADS * Q_BLOCK), F32))
    o = ot_ref[...] / jnp.sum(l, axis=0, keepdims=True)
    o = jnp.concatenate([o[:, lanes[h]] for h in heads], axis=0)
    o_ref[0] = o.T.astype(BF16)


def _dsa(qt, qit, wt, k, ki, vt):
    bsz, n_blk, _, hq = qt.shape
    seq = n_blk * Q_BLOCK
    n_sel = min(TOPK_MAX, seq // 4)
    n_kc = seq // KEY_CHUNK
    hd = A_HEADS * A_HEAD_DIM
    kern = functools.partial(_dsa_kernel, n_sel=n_sel, search_steps=32)
    return pl.pallas_call(
        kern,
        grid=(bsz, n_blk),
        in_specs=[pl.BlockSpec((1, 1, A_HEAD_DIM, hq), lambda b, i: (b, i, 0, 0)),
                  pl.BlockSpec((1, 1, IDX_DIM, hq), lambda b, i: (b, i, 0, 0)),
                  pl.BlockSpec((1, IDX_HEADS, Q_BLOCK), lambda b, i: (b, 0, i)),
                  pl.BlockSpec((1, seq, A_HEAD_DIM), lambda b, i: (b, 0, 0)),
                  pl.BlockSpec((1, seq, IDX_DIM), lambda b, i: (b, 0, 0)),
                  pl.BlockSpec((1, n_kc, A_HEAD_DIM, KEY_CHUNK), lambda b, i: (b, 0, 0, 0))],
        out_specs=pl.BlockSpec((1, Q_BLOCK, hd), lambda b, i: (b, i, 0)),
        out_shape=jax.ShapeDtypeStruct((bsz, seq, hd), BF16),
        scratch_shapes=[pltpu.VMEM((n_kc, KEY_CHUNK, Q_BLOCK), F32),
                        pltpu.VMEM((n_kc, KEY_CHUNK, hq), F32),
                        pltpu.VMEM((A_HEAD_DIM, hq), F32)],
        compiler_params=_params("arbitrary", "arbitrary"),
        name="dsa",
    )(qt, qit, wt, k, ki, vt)


def _softplus(x):
    return jnp.maximum(x, 0.0) + jnp.log(1.0 + jnp.exp(-jnp.abs(x)))


def _split3(x):
    hi = x.astype(BF16)
    r = x - hi.astype(F32)
    mid = r.astype(BF16)
    lo = (r - mid.astype(F32)).astype(BF16)
    return hi, mid, lo


def _gdn_kernel(q_ref, k_ref, v_ref, z_ref, ab_ref, cwq_ref, cwk_ref, cwv_ref, alog_ref, dtb_ref, ng_ref,
                o_ref, st_ref, *, n_steps, heads):
    c = GDN_CHUNK
    dk = B_KEY_DIM
    halo = SUBLANES

    def conv_head(src_ref, w_ref, r0, prev0, live, h):
        cols = slice(h * dk, (h + 1) * dk)
        cur = src_ref[0, pl.ds(r0, c), cols]
        prev = src_ref[0, pl.ds(prev0, halo), cols] * live
        xw = jnp.concatenate([prev, cur], axis=0)
        y = cur * w_ref[CONV_WIDTH - 1:CONV_WIDTH, cols]
        for j in range(CONV_WIDTH - 1):
            sh = CONV_WIDTH - 1 - j
            y = y + xw[halo - sh:halo - sh + c] * w_ref[j:j + 1, cols]
        return _silu(y)

    def l2n(t):
        return t * lax.rsqrt(jnp.sum(t * t, axis=-1, keepdims=True) + RMS_EPS)

    st_ref[...] = jnp.zeros(st_ref.shape, F32)
    ri = lax.broadcasted_iota(jnp.int32, (c, c), 0)
    ci = lax.broadcasted_iota(jnp.int32, (c, c), 1)
    strict = ri > ci
    eye = jnp.where(ri == ci, 1.0, 0.0)
    incl_ones = jnp.where(ri >= ci, 1.0, 0.0).astype(BF16)
    inv_shift = GDN_INV_BLOCK.bit_length() - 1
    diag_mask = jnp.where((ri >> inv_shift) == (ci >> inv_shift), 1.0, 0.0)
    level_masks = []
    for ls in range(inv_shift, c.bit_length() - 1):
        lower_left = jnp.where((ri >> ls) == (ci >> ls) + 1, 1.0, 0.0) * jnp.where(((ci >> ls) & 1) == 0, 1.0, 0.0)
        level_masks.append(lower_left.astype(BF16))

    def step(s, carry):
        r0 = pl.multiple_of(s * c, c)
        prev0 = pl.multiple_of(jnp.maximum(r0 - halo, 0), halo)
        live = jnp.where(s > 0, 1.0, 0.0)
        rows = pl.ds(r0, c)
        ab = ab_ref[0, rows, :]
        log_a = -jnp.exp(alog_ref[...]) * _softplus(ab + dtb_ref[...])
        beta_all = 1.0 / (1.0 + jnp.exp(-ab))
        la_hi, la_mid, la_lo = _split3(log_a)
        gam_all = _dot(incl_ones, la_hi) + _dot(incl_ones, la_mid) + _dot(incl_ones, la_lo)

        hs = range(heads)
        gam = [gam_all[:, h:h + 1] for h in hs]
        beta = [beta_all[:, B_HEADS + h:B_HEADS + h + 1] for h in hs]
        cols = [slice(h * dk, (h + 1) * dk) for h in hs]
        kh = [l2n(conv_head(k_ref, cwk_ref, r0, prev0, live, h)) for h in hs]
        qh = [l2n(conv_head(q_ref, cwq_ref, r0, prev0, live, h)) * (dk ** -0.5) for h in hs]
        vh = [conv_head(v_ref, cwv_ref, r0, prev0, live, h) for h in hs]
        kb = [kh[h].astype(BF16) for h in hs]
        kk = [_dot_nt(kb[h], kb[h]) for h in hs]
        qk = [_dot_nt(qh[h].astype(BF16), kb[h]) for h in hs]

        dec = []
        for h in hs:
            gcol = jnp.broadcast_to(gam[h], (c, c))
            diff = gcol - gcol.T
            dec.append(jnp.where(strict, jnp.exp(jnp.where(strict, diff, 0.0)), 0.0))
        nmat = [beta[h] * kk[h] * dec[h] for h in hs]
        a_qk = [(qk[h] * (dec[h] + eye)).astype(BF16) for h in hs]

        mp = [-(nmat[h] * diag_mask) for h in hs]
        x = [eye + mp[h] for h in hs]
        for _ in range(GDN_INV_BLOCK.bit_length() - 2):
            mpb = [mp[h].astype(BF16) for h in hs]
            mp = [_dot(mpb[h], mpb[h]) for h in hs]
            x = [x[h] + _dot(x[h].astype(BF16), mp[h].astype(BF16)) for h in hs]
        nb = [nmat[h].astype(BF16) for h in hs]
        for lm in level_masks:
            xb = [x[h].astype(BF16) for h in hs]
            cx = [_dot(nb[h] * lm, xb[h]).astype(BF16) for h in hs]
            x = [x[h] - _dot(xb[h], cx[h]) for h in hs]
        xb = [x[h].astype(BF16) for h in hs]

        g_cum = [jnp.exp(gam[h]) for h in hs]
        w = [_dot(xb[h], ((beta[h] * g_cum[h]) * kh[h]).astype(BF16)).astype(BF16) for h in hs]
        u = [_dot(xb[h], (beta[h] * vh[h]).astype(BF16)) for h in hs]
        g_last = [gam[h][c - 1:c, :] for h in hs]
        q_dec = [(qh[h] * g_cum[h]).astype(BF16) for h in hs]
        k_dec = [(kh[h] * jnp.exp(g_last[h] - gam[h])).astype(BF16) for h in hs]

        state = [st_ref[h] for h in hs]
        sb = [state[h].astype(BF16) for h in hs]
        db = [(u[h] - _dot(w[h], sb[h])).astype(BF16) for h in hs]
        out = [_dot(q_dec[h], sb[h]) + _dot(a_qk[h], db[h]) for h in hs]
        for h in hs:
            st_ref[h] = jnp.exp(g_last[h]) * state[h] + _dot_tn(k_dec[h], db[h])
        for h in hs:
            normed = (out[h] * lax.rsqrt(jnp.mean(out[h] * out[h], axis=-1, keepdims=True) + RMS_EPS)
                      * ng_ref[...])
            o_ref[0, rows, cols[h]] = (normed * _silu(z_ref[0, rows, cols[h]])).astype(BF16)
        return carry

    lax.fori_loop(0, n_steps, step, 0)


def _gdn(qkv, z, ab, conv_w, a_log, dt_bias, norm_g):
    bsz, seq, _ = qkv.shape
    heads = B_HEADS
    gw = heads * B_KEY_DIM
    pad_lanes = lambda v: jnp.pad(v.reshape(1, -1), ((0, 0), (0, LANES - v.size)))
    kern = functools.partial(_gdn_kernel, n_steps=seq // GDN_CHUNK, heads=heads)
    col_block = lambda off: (lambda b: (b, 0, off))
    w_block = lambda off: (lambda b: (0, off))
    const2 = lambda b: (0, 0)
    return pl.pallas_call(
        kern,
        grid=(bsz,),
        in_specs=[pl.BlockSpec((1, seq, gw), col_block(0)),
                  pl.BlockSpec((1, seq, gw), col_block(1)),
                  pl.BlockSpec((1, seq, gw), col_block(2)),
                  pl.BlockSpec((1, seq, gw), col_block(0)),
                  pl.BlockSpec((1, seq, LANES), lambda b: (b, 0, 0)),
                  pl.BlockSpec((CONV_WIDTH, gw), w_block(0)),
                  pl.BlockSpec((CONV_WIDTH, gw), w_block(1)),
                  pl.BlockSpec((CONV_WIDTH, gw), w_block(2)),
                  pl.BlockSpec((1, LANES), const2),
                  pl.BlockSpec((1, LANES), const2),
                  pl.BlockSpec((1, B_VAL_DIM), const2)],
        out_specs=pl.BlockSpec((1, seq, gw), col_block(0)),
        out_shape=jax.ShapeDtypeStruct((bsz, seq, B_HEADS * B_VAL_DIM), BF16),
        scratch_shapes=[pltpu.VMEM((heads, B_KEY_DIM, B_VAL_DIM), F32)],
        compiler_params=_params("arbitrary"),
        name="gdn",
    )(qkv, qkv, qkv, z, ab, conv_w, conv_w, conv_w, pad_lanes(a_log), pad_lanes(dt_bias),
      norm_g.reshape(1, B_VAL_DIM))


def _outproj_kernel(x_ref, mod_ref, attn_ref, dn_ref, wo_ref, g_ref, b_ref, o_ref, *, n_out_chunks):
    gate = mod_ref[0, 5:6, :]
    rows_per = x_ref.shape[1] // n_out_chunks
    for r in range(n_out_chunks):
        rows = slice(r * rows_per, (r + 1) * rows_per)
        mixed = jnp.concatenate([attn_ref[0, rows, :], dn_ref[0, rows, :]], axis=1)
        y = DEEPNORM_ALPHA * x_ref[0, rows, :] + gate * _dot(mixed, wo_ref[...])
        o_ref[0, rows, :] = _layer_norm(y, g_ref[...], b_ref[...])


def _outproj(x, mod, attn, dn, w_out, ln_g, ln_b):
    bsz, seq, d = x.shape
    tm = min(OUT_TOKENS, seq)
    tok = lambda b, i: (b, i, 0)
    const2 = lambda b, i: (0, 0)
    kern = functools.partial(_outproj_kernel, n_out_chunks=max(1, tm // FFN_OUT_ROWS))
    return pl.pallas_call(
        kern,
        grid=(bsz, seq // tm),
        in_specs=[pl.BlockSpec((1, tm, d), tok),
                  pl.BlockSpec((1, N_MOD, d), lambda b, i: (b, 0, 0)),
                  pl.BlockSpec((1, tm, attn.shape[2]), tok),
                  pl.BlockSpec((1, tm, dn.shape[2]), tok),
                  pl.BlockSpec(w_out.shape, const2),
                  pl.BlockSpec((1, d), const2),
                  pl.BlockSpec((1, d), const2)],
        out_specs=pl.BlockSpec((1, tm, d), tok),
        out_shape=jax.ShapeDtypeStruct((bsz, seq, d), F32),
        compiler_params=_params("arbitrary", "arbitrary"),
        name="outproj",
    )(x, mod, attn, dn, w_out.astype(BF16), ln_g.reshape(1, d), ln_b.reshape(1, d))


def kernel(x, c, positions, w_ada, b_ada, ffn1_w1, ffn1_w3, ffn1_w2, ln1_g, ln1_b, w_in, conv_w, a_log, dt_bias, dn_norm_g, w_out, ln2_g, ln2_b, ffn2_w1, ffn2_w3, ffn2_w2, ln3_g, ln3_b):
    bsz, seq, d = x.shape
    pos_f = positions.astype(F32).reshape(bsz, 1, seq)
    for layer in range(w_ada.shape[0]):
        mod = _adaln(c, w_ada[layer], b_ada[layer]).reshape(bsz, N_MOD, d)
        x = _ffn_block(x, mod, ffn1_w1[layer], ffn1_w3[layer], ffn1_w2[layer], ln1_g[layer], ln1_b[layer],
                       mod_base=0)
        qt, qit, wt, k, ki, vt, qkv, z, ab = _proj(x, mod, pos_f, w_in[layer])
        attn = _dsa(qt, qit, wt, k, ki, vt)
        dn = _gdn(qkv, z, ab, conv_w[layer], a_log[layer], dt_bias[layer], dn_norm_g[layer])
        x = _outproj(x, mod, attn, dn, w_out[layer], ln2_g[layer], ln2_b[layer])
        x = _ffn_block(x, mod, ffn2_w1[layer], ffn2_w3[layer], ffn2_w2[layer], ln3_g[layer], ln3_b[layer],
                       mod_base=6)
    return x
```

```python
import functools

import jax
import jax.numpy as jnp
from jax import lax
from jax.experimental import pallas as pl
from jax.experimental.pallas import tpu as pltpu

F32 = jnp.float32
BF16 = jnp.bfloat16

A_HEADS = 8
A_HEAD_DIM = 64
IDX_HEADS = 8
IDX_DIM = 64
TOPK_MAX = 256
ROPE_THETA = 10000.0
B_HEADS = 4
B_KEY_DIM = 128
B_VAL_DIM = 128
CONV_WIDTH = 4
MACARON_WEIGHT = 0.5
DEPTH = 1
DEEPNORM_ALPHA = (2.0 * DEPTH) ** 0.25
LN_EPS = 1e-5
RMS_EPS = 1e-6
N_MOD = 9

LANES = 128
SUBLANES = 8
VMEM_LIMIT_BYTES = 56 * 1024 * 1024

FFN_TOKENS = 1024
FFN_CHUNK = 256
FFN_OUT_ROWS = 512
PROJ_TOKENS = 512
PROJ_CONV_COLS = 256
PROJ_CONV_ROWS = 128
OUT_TOKENS = 1024
ROW_CHUNK = 256
Q_BLOCK = 256
KEY_CHUNK = 256
GDN_CHUNK = 256
GDN_HEADS_PER_STEP = 2
GDN_INV_BLOCK = 16

COUNT_ROWS = 32

NEG_INF = float("-inf")


def _silu(x):
    return x * (1.0 / (1.0 + jnp.exp(-x)))


def _dot(a, b):
    return jnp.dot(a, b, preferred_element_type=F32)


def _dot_nt(a, b):
    return lax.dot_general(a, b, (((1,), (1,)), ((), ())), preferred_element_type=F32)


def _dot_tn(a, b):
    return lax.dot_general(a, b, (((0,), (0,)), ((), ())), preferred_element_type=F32)


def _layer_norm(y, g, b):
    mu = jnp.mean(y, axis=-1, keepdims=True)
    d = y - mu
    var = jnp.mean(d * d, axis=-1, keepdims=True)
    return d * lax.rsqrt(var + LN_EPS) * g + b


def _params(*sem):
    return pltpu.CompilerParams(dimension_semantics=sem, vmem_limit_bytes=VMEM_LIMIT_BYTES)


def _adaln_kernel(c_ref, w_ref, b_ref, o_ref):
    sc = _silu(c_ref[...]).astype(BF16)
    o_ref[...] = _dot(sc, w_ref[...].astype(BF16)) + b_ref[...]


def _adaln(c, w, b):
    bsz, d = c.shape
    n = w.shape[1]
    tn = n // 8
    return pl.pallas_call(
        _adaln_kernel,
        grid=(n // tn,),
        in_specs=[pl.BlockSpec((bsz, d), lambda j: (0, 0)),
                  pl.BlockSpec((d, tn), lambda j: (0, j)),
                  pl.BlockSpec((1, tn), lambda j: (0, j))],
        out_specs=pl.BlockSpec((bsz, tn), lambda j: (0, j)),
        out_shape=jax.ShapeDtypeStruct((bsz, n), F32),
        compiler_params=_params("arbitrary"),
        name="adaln",
    )(c, w, b.reshape(1, n))


def _ffn_kernel(x_ref, mod_ref, w1_ref, w3_ref, w2_ref, g_ref, b_ref, o_ref, u_ref, hid_ref,
                *, mod_base, n_chunks, n_row_chunks, n_out_chunks):
    shift = mod_ref[0, mod_base:mod_base + 1, :]
    scale = mod_ref[0, mod_base + 1:mod_base + 2, :]
    gate = mod_ref[0, mod_base + 2:mod_base + 3, :]

    def prologue(r, carry):
        rows = pl.ds(pl.multiple_of(r * ROW_CHUNK, ROW_CHUNK), ROW_CHUNK)
        u_ref[rows, :] = (x_ref[0, rows, :] * (1.0 + scale) + shift).astype(BF16)
        return carry

    lax.fori_loop(0, n_row_chunks, prologue, 0)

    u = u_ref[...]
    for f in range(n_chunks):
        cols = slice(f * FFN_CHUNK, (f + 1) * FFN_CHUNK)
        a = _dot(u, w1_ref[:, cols])
        b = _dot(u, w3_ref[:, cols])
        hid_ref[:, cols] = (_silu(a) * b).astype(BF16)

    rows_per = x_ref.shape[1] // n_out_chunks
    for r in range(n_out_chunks):
        rows = slice(r * rows_per, (r + 1) * rows_per)
        h = _dot(hid_ref[rows, :], w2_ref[...])
        y =DEEPNORM_ALPHA * x_ref[0, rows, :] + (MACARON_WEIGHT * gate) * h
        o_ref[0, rows, :] = _layer_norm(y, g_ref[...], b_ref[...])


def _ffn_block(x, mod, w1, w3, w2, ln_g, ln_b, *, mod_base):
    bsz, seq, d = x.shape
    dff = w1.shape[1]
    n_chunks = dff // FFN_CHUNK
    tm = min(FFN_TOKENS, seq)
    kern = functools.partial(_ffn_kernel, mod_base=mod_base, n_chunks=n_chunks,
                             n_row_chunks=tm // ROW_CHUNK, n_out_chunks=max(1, tm // FFN_OUT_ROWS))
    const2 = lambda b, i: (0, 0)
    return pl.pallas_call(
        kern,
        grid=(bsz, seq // tm),
        in_specs=[pl.BlockSpec((1, tm, d), lambda b, i: (b, i, 0)),
                  pl.BlockSpec((1, N_MOD, d), lambda b, i: (b, 0, 0)),
                  pl.BlockSpec((d, dff), const2, pipeline_mode=pl.Buffered(1)),
                  pl.BlockSpec((d, dff), const2, pipeline_mode=pl.Buffered(1)),
                  pl.BlockSpec((dff, d), const2, pipeline_mode=pl.Buffered(1)),
                  pl.BlockSpec((1, d), lambda b, i: (0, 0)),
                  pl.BlockSpec((1, d), lambda b, i: (0, 0))],
        out_specs=pl.BlockSpec((1, tm, d), lambda b, i: (b, i, 0)),
        out_shape=jax.ShapeDtypeStruct((bsz, seq, d), F32),
        scratch_shapes=[pltpu.VMEM((tm, d), BF16), pltpu.VMEM((tm, dff), BF16)],
        compiler_params=_params("arbitrary", "arbitrary"),
        name="ffn",
    )(x, mod, w1.astype(BF16), w3.astype(BF16), w2.astype(BF16), ln_g.reshape(1, d), ln_b.reshape(1, d))


_ROW_Q = 0
_ROW_QI = _ROW_Q + A_HEADS * A_HEAD_DIM
_ROW_K = _ROW_QI + IDX_HEADS * IDX_DIM
_ROW_KI = _ROW_K + A_HEAD_DIM
_ROW_V = _ROW_KI + IDX_DIM
_ROW_W = _ROW_V + A_HEAD_DIM
_ROWS_A = _ROW_W + IDX_HEADS
_ROWS_A_PADDED = -(-_ROWS_A // 16) * 16
_COL_QKV = 0
_COL_Z = 2 * B_HEADS * B_KEY_DIM + B_HEADS * B_VAL_DIM
_COL_AB = _COL_Z + B_HEADS * B_VAL_DIM
_COLS_B = _COL_AB + LANES


def _proj_kernel(x_ref, mod_ref, pos_ref, invf_ref, wat_ref, wb_ref, cw_ref,
                 qt_ref, qit_ref, wt_ref, k_ref, ki_ref, vt_ref, qkv_ref, z_ref, ab_ref,
                 u_ref, pat_ref, prev_ref, *, n_row_chunks):
    shift = mod_ref[0, 3:4, :]
    scale = mod_ref[0, 4:5, :]
    tm = x_ref.shape[1]

    def prologue(r, carry):
        rows = pl.ds(pl.multiple_of(r * ROW_CHUNK, ROW_CHUNK), ROW_CHUNK)
        u_ref[rows, :] = (x_ref[0, rows, :] * (1.0 + scale) + shift).astype(BF16)
        return carry

    lax.fori_loop(0, n_row_chunks, prologue, 0)
    u = u_ref[...]

    halo = SUBLANES

    @pl.when(pl.program_id(1) == 0)
    def _():
        prev_ref[...] = jnp.zeros(prev_ref.shape, F32)

    n_normed = 2 * B_HEADS * B_KEY_DIM
    rb = PROJ_CONV_ROWS
    for c0 in range(_COL_QKV, _COL_Z, PROJ_CONV_COLS):
        cols = slice(c0, c0 + PROJ_CONV_COLS)
        prev = prev_ref[:, cols]
        for r0 in range(0, tm, rb):
            raw = _dot(u_ref[r0:r0 + rb, :], wb_ref[:, cols])
            xw = jnp.concatenate([prev, raw], axis=0)
            prev = raw[rb - halo:, :]
            y = raw * cw_ref[CONV_WIDTH - 1:CONV_WIDTH, cols]
            for j in range(CONV_WIDTH - 1):
                sh = CONV_WIDTH - 1 - j
                y = y + xw[halo - sh:halo - sh + rb] * cw_ref[j:j + 1, cols]
            y = _silu(y)
            for h0 in range(0, PROJ_CONV_COLS, B_KEY_DIM):
                t = y[:, h0:h0 + B_KEY_DIM]
                if c0 + h0 < n_normed:
                    t = t * lax.rsqrt(jnp.sum(t * t, axis=-1, keepdims=True) + RMS_EPS)
                qkv_ref[0, r0:r0 + rb, c0 + h0:c0 + h0 + B_KEY_DIM] = t
        prev_ref[:, cols] = prev
    z_ref[0] = _dot(u, wb_ref[:, _COL_Z:_COL_AB])
    ab_ref[0] = _dot(u, wb_ref[:, _COL_AB:_COLS_B])

    ang = invf_ref[...] * pos_ref[0]
    cos = jnp.cos(ang)
    sin = jnp.sin(ang)
    half = A_HEAD_DIM // 2

    pat_ref[...] = _dot_nt(wat_ref[...], u)

    def roped(row0):
        t1 = pat_ref[row0:row0 + half, :]
        t2 = pat_ref[row0 + half:row0 + A_HEAD_DIM, :]
        return t1 * cos - t2 * sin, t2 * cos + t1 * sin

    def store_heads(dst_ref, row_base, scale):
        for h in range(A_HEADS):
            o1, o2 = roped(row_base + h * A_HEAD_DIM)
            o1 = (o1 * scale).astype(BF16)
            o2 = (o2 * scale).astype(BF16)
            lanes = slice(h * Q_BLOCK, (h + 1) * Q_BLOCK)
            for j in range(dst_ref.shape[1]):
                toks = slice(j * Q_BLOCK, (j + 1) * Q_BLOCK)
                dst_ref[0, j, :half, lanes] = o1[:, toks]
                dst_ref[0, j, half:, lanes] = o2[:, toks]

    store_heads(qt_ref, _ROW_Q, A_HEAD_DIM ** -0.5)
    store_heads(qit_ref, _ROW_QI, 1.0)

    k1, k2 = roped(_ROW_K)
    ki1, ki2 = roped(_ROW_KI)
    kk = jnp.concatenate([k1, k2, ki1, ki2], axis=0).T
    k_ref[0] = kk[:, :A_HEAD_DIM].astype(BF16)
    ki_ref[0] = kk[:, A_HEAD_DIM:].astype(BF16)

    for j in range(vt_ref.shape[1]):
        vt_ref[0, j] = pat_ref[_ROW_V:_ROW_V + A_HEAD_DIM, j * KEY_CHUNK:(j + 1) * KEY_CHUNK].astype(BF16)

    wt_ref[0] = pat_ref[_ROW_W:_ROW_W + IDX_HEADS, :] * (IDX_HEADS ** -0.5 * IDX_DIM ** -0.5)


def _proj(x, mod, pos_f, w_in, conv_w):
    bsz, seq, d = x.shape
    tm = min(PROJ_TOKENS, seq)
    splits = (A_HEADS * A_HEAD_DIM, A_HEAD_DIM, A_HEAD_DIM, IDX_HEADS * IDX_DIM, IDX_DIM, IDX_HEADS,
              B_HEADS * B_KEY_DIM, B_HEADS * B_KEY_DIM, B_HEADS * B_VAL_DIM, B_HEADS * B_VAL_DIM,
              B_HEADS, B_HEADS)
    offs = [0]
    for s in splits:
        offs.append(offs[-1] + s)
    cols = [w_in[:, offs[i]:offs[i + 1]] for i in range(len(splits))]
    a_q, a_k, a_v, i_q, i_k, i_w, b_q, b_k, b_v, b_z, b_a, b_b = cols
    wat = jnp.concatenate([a_q, i_q, a_k, i_k, a_v, i_w], axis=1).T.astype(BF16)
    wat = jnp.pad(wat, ((0, _ROWS_A_PADDED - _ROWS_A), (0, 0)))
    wb = jnp.concatenate([b_q, b_k, b_v, b_z, b_a, b_b], axis=1).astype(BF16)
    wb = jnp.pad(wb, ((0, 0), (0, _COLS_B - wb.shape[1])))
    half = A_HEAD_DIM // 2
    inv_freq = (ROPE_THETA ** (-jnp.arange(0, A_HEAD_DIM, 2, dtype=F32) / A_HEAD_DIM)).reshape(half, 1)

    n_kc = tm // KEY_CHUNK
    qkv_w = _COL_Z
    z_w = _COL_AB - _COL_Z
    tok = lambda b, i: (b, i, 0)
    tok_t = lambda b, i: (b, 0, i)
    const2 = lambda b, i: (0, 0)
    n_qb = tm // Q_BLOCK
    q_spec = pl.BlockSpec((1, n_qb, A_HEAD_DIM, A_HEADS * Q_BLOCK), lambda b, i: (b, i, 0, 0))
    out_shapes = (
        jax.ShapeDtypeStruct((bsz, seq // Q_BLOCK, A_HEAD_DIM, A_HEADS * Q_BLOCK), BF16),
        jax.ShapeDtypeStruct((bsz, seq // Q_BLOCK, IDX_DIM, IDX_HEADS * Q_BLOCK), BF16),
        jax.ShapeDtypeStruct((bsz, IDX_HEADS, seq), F32),
        jax.ShapeDtypeStruct((bsz, seq, A_HEAD_DIM), BF16),
        jax.ShapeDtypeStruct((bsz, seq, IDX_DIM), BF16),
        jax.ShapeDtypeStruct((bsz, seq // KEY_CHUNK, A_HEAD_DIM, KEY_CHUNK), BF16),
        jax.ShapeDtypeStruct((bsz, seq, qkv_w), F32),
        jax.ShapeDtypeStruct((bsz, seq, z_w), F32),
        jax.ShapeDtypeStruct((bsz, seq, LANES), F32),
    )
    out_specs = (
        q_spec,
        q_spec,
        pl.BlockSpec((1, IDX_HEADS, tm), tok_t),
        pl.BlockSpec((1, tm, A_HEAD_DIM), tok),
        pl.BlockSpec((1, tm, IDX_DIM), tok),
        pl.BlockSpec((1, n_kc, A_HEAD_DIM, KEY_CHUNK), lambda b, i: (b, i, 0, 0)),
        pl.BlockSpec((1, tm, qkv_w), tok),
        pl.BlockSpec((1, tm, z_w), tok),
        pl.BlockSpec((1, tm, LANES), tok),
    )
    kern = functools.partial(_proj_kernel, n_row_chunks=tm // ROW_CHUNK)
    return pl.pallas_call(
        kern,
        grid=(bsz, seq // tm),
        in_specs=[pl.BlockSpec((1, tm, d), tok),
                  pl.BlockSpec((1, N_MOD, d), lambda b, i: (b, 0, 0)),
                  pl.BlockSpec((1, 1, tm), tok_t),
                  pl.BlockSpec((half, 1), const2),
                  pl.BlockSpec((_ROWS_A_PADDED, d), const2),
                  pl.BlockSpec((d, _COLS_B), const2),
                  pl.BlockSpec(conv_w.shape, const2)],
        out_specs=out_specs,
        out_shape=out_shapes,
        scratch_shapes=[pltpu.VMEM((tm, d), BF16), pltpu.VMEM((_ROWS_A_PADDED, tm), F32),
                        pltpu.VMEM((SUBLANES, qkv_w), F32)],
        compiler_params=_params("arbitrary", "arbitrary"),
        name="proj",
    )(x, mod, pos_f, inv_freq, wat, wb, conv_w)


_KEY_LO = -2139095041
_KEY_HI = 2139095041


def _key_to_float(key):
    bits = jnp.where(key < 0, key ^ 0x7FFFFFFF, key)
    return lax.bitcast_convert_type(bits, F32)


def _fold_rows(t):
    return t.reshape(t.shape[0] // SUBLANES, SUBLANES, t.shape[1])


def _dsa_kernel(qt_ref, qit_ref, wt_ref, k_ref, ki_ref, vt_ref, o_ref, s_ref, lg_ref, ot_ref,
                *, n_sel, search_steps):
    blk = pl.program_id(1)
    kc = KEY_CHUNK
    q_base = blk * Q_BLOCK
    n_keys = q_base + Q_BLOCK
    n_kc = (n_keys + kc - 1) // kc
    row = lax.broadcasted_iota(jnp.int32, (kc, Q_BLOCK), 0)
    col = lax.broadcasted_iota(jnp.int32, (kc, Q_BLOCK), 1)
    wt = wt_ref[0]
    heads = range(A_HEADS)
    lanes = [slice(h * Q_BLOCK, (h + 1) * Q_BLOCK) for h in heads]

    def chunk_rows(c):
        return pl.ds(pl.multiple_of(c * kc, kc), kc)

    def score_chunk(c, carry):
        x = _dot(ki_ref[0, chunk_rows(c), :], qit_ref[0, 0])
        acc = wt[0:1, :] * jnp.maximum(x[:, lanes[0]], 0.0)
        for h in range(1, IDX_HEADS):
            acc = acc + wt[h:h + 1, :] * jnp.maximum(x[:, lanes[h]], 0.0)
        causal = row + (c * kc - q_base) <= col
        s_ref[c] = jnp.where(causal, acc, NEG_INF)
        return carry

    lax.fori_loop(0, n_kc, score_chunk, 0)

    def count(pred):
        def body(c, acc):
            ind = jnp.where(pred(s_ref[c]), 1.0, 0.0)
            return acc + jnp.sum(ind.reshape(kc // COUNT_ROWS, COUNT_ROWS, Q_BLOCK), axis=0)

        acc = lax.fori_loop(0, n_kc, body, jnp.zeros((COUNT_ROWS, Q_BLOCK), F32))
        return jnp.sum(acc, axis=0, keepdims=True)

    def rewrite_scores(fn):
        def body(c, carry):
            s_ref[c] = fn(s_ref[c])
            return carry
        lax.fori_loop(0, n_kc, body, 0)

    @pl.when(n_keys <= n_sel)
    def _():
        rewrite_scores(lambda s: jnp.where(s > NEG_INF, 0.0, NEG_INF))

    @pl.when(n_keys > n_sel)
    def _():
        def bisect(i, carry):
            lo, hi, cnt_lo = carry
            mid = (lo & hi) + ((lo ^ hi) >> 1)
            thr = _key_to_float(mid)
            cnt = count(lambda s: s >= thr)
            ok = cnt >= n_sel
            return jnp.where(ok, mid, lo), jnp.where(ok, hi, mid), jnp.where(ok, cnt, cnt_lo)

        lo0 = jnp.full((1, Q_BLOCK), _KEY_LO, jnp.int32)
        hi0 = jnp.full((1, Q_BLOCK), _KEY_HI, jnp.int32)
        cnt0 = jnp.full((1, Q_BLOCK), float(n_sel), F32)
        lo, _, cnt_lo = lax.fori_loop(0, search_steps, bisect, (lo0, hi0, cnt0))
        thr = _key_to_float(lo)
        has_ties = jnp.max(cnt_lo) > n_sel

        @pl.when(jnp.logical_not(has_ties))
        def _():
            rewrite_scores(lambda s: jnp.where(s >= thr, 0.0, NEG_INF))

        @pl.when(has_ties)
        def _():
            quota = n_sel - count(lambda s: s > thr)
            lower = (lax.broadcasted_iota(jnp.int32, (kc, kc), 1)
                     < lax.broadcasted_iota(jnp.int32, (kc, kc), 0))
            lower = jnp.where(lower, 1.0, 0.0).astype(BF16)

            def body(c, seen):
                s = s_ref[c]
                eq = jnp.where(s == thr, 1.0, 0.0)
                rank = _dot(lower, eq.astype(BF16)) + seen
                take = jnp.where(s > thr, 1.0, jnp.where(rank < quota, eq, 0.0))
                s_ref[c] = jnp.where(take > 0.5, 0.0, NEG_INF)
                return seen + jnp.sum(eq, axis=0, keepdims=True)

            lax.fori_loop(0, n_kc, body, jnp.zeros((1, Q_BLOCK), F32))

    def logits_chunk(c, m):
        bias = s_ref[c]
        lg = _dot(k_ref[0, chunk_rows(c), :], qt_ref[0, 0]) + jnp.concatenate([bias] * A_HEADS, axis=1)
        lg_ref[c] = lg
        return jnp.maximum(m, jnp.max(_fold_rows(lg), axis=0))

    m = lax.fori_loop(0, n_kc, logits_chunk, jnp.full((SUBLANES, A_HEADS * Q_BLOCK), NEG_INF, F32))
    m = jnp.max(m, axis=0, keepdims=True)

    ot_ref[...] = jnp.zeros(ot_ref.shape, F32)

    def pv_chunk(c, l):
        p = jnp.exp(lg_ref[c] - m)
        ot_ref[...] += _dot(vt_ref[0, c], p.astype(BF16))
        return l + jnp.sum(_fold_rows(p), axis=0)

    l = lax.fori_loop(0, n_kc, pv_chunk, jnp.zeros((SUBLANES, A_HEADS * Q_BLOCK), F32))
    o = ot_ref[...] / jnp.sum(l, axis=0, keepdims=True)
    o = jnp.concatenate([o[:, lanes[h]] for h in heads], axis=0)
    o_ref[0] = o.T.astype(BF16)


def _dsa(qt, qit, wt, k, ki, vt):
    bsz, n_blk, _, hq = qt.shape
    seq = n_blk * Q_BLOCK
    n_sel = min(TOPK_MAX, seq // 4)
    n_kc = seq // KEY_CHUNK
    hd = A_HEADS * A_HEAD_DIM
    kern = functools.partial(_dsa_kernel, n_sel=n_sel, search_steps=32)
    return pl.pallas_call(
        kern,
        grid=(bsz, n_blk),
        in_specs=[pl.BlockSpec((1, 1, A_HEAD_DIM, hq), lambda b, i: (b, i, 0, 0)),
                  pl.BlockSpec((1, 1, IDX_DIM, hq), lambda b, i: (b, i, 0, 0)),
                  pl.BlockSpec((1, IDX_HEADS, Q_BLOCK), lambda b, i: (b, 0, i)),
                  pl.BlockSpec((1, seq, A_HEAD_DIM), lambda b, i: (b, 0, 0)),
                  pl.BlockSpec((1, seq, IDX_DIM), lambda b, i: (b, 0, 0)),
                  pl.BlockSpec((1, n_kc, A_HEAD_DIM, KEY_CHUNK), lambda b, i: (b, 0, 0, 0))],
        out_specs=pl.BlockSpec((1, Q_BLOCK, hd), lambda b, i: (b, i, 0)),
        out_shape=jax.ShapeDtypeStruct((bsz, seq, hd), BF16),
        scratch_shapes=[pltpu.VMEM((n_kc, KEY_CHUNK, Q_BLOCK), F32),
                        pltpu.VMEM((n_kc, KEY_CHUNK, hq), F32),
                        pltpu.VMEM((A_HEAD_DIM, hq), F32)],
        compiler_params=_params("arbitrary", "arbitrary"),
        name="dsa",
    )(qt, qit, wt, k, ki, vt)


def _softplus(x):
    return jnp.maximum(x, 0.0) + jnp.log(1.0 + jnp.exp(-jnp.abs(x)))


def _split3(x):
    hi = x.astype(BF16)
    r = x - hi.astype(F32)
    mid = r.astype(BF16)
    lo = (r - mid.astype(F32)).astype(BF16)
    return hi, mid, lo


def _gdn_kernel(q_ref, k_ref, v_ref, z_ref, ab_ref, alog_ref, dtb_ref, ng_ref,
                o_ref, st_ref, *, n_steps, heads):
    c = GDN_CHUNK
    dk = B_KEY_DIM

    st_ref[...] = jnp.zeros(st_ref.shape, F32)
    ri = lax.broadcasted_iota(jnp.int32, (c, c), 0)
    ci = lax.broadcasted_iota(jnp.int32, (c, c), 1)
    strict = ri > ci
    eye = jnp.where(ri == ci, 1.0, 0.0)
    incl_ones = jnp.where(ri >= ci, 1.0, 0.0).astype(BF16)
    inv_shift = GDN_INV_BLOCK.bit_length() - 1
    diag_mask = jnp.where((ri >> inv_shift) == (ci >> inv_shift), 1.0, 0.0)
    level_masks = []
    for ls in range(inv_shift, c.bit_length() - 1):
        lower_left = jnp.where((ri >> ls) == (ci >> ls) + 1, 1.0, 0.0) * jnp.where(((ci >> ls) & 1) == 0, 1.0, 0.0)
        level_masks.append(lower_left.astype(BF16))

    def step(s, carry):
        rows = pl.ds(pl.multiple_of(s * c, c), c)
        ab = ab_ref[0, rows, :]
        log_a = -jnp.exp(alog_ref[...]) * _softplus(ab + dtb_ref[...])
        beta_all = 1.0 / (1.0 + jnp.exp(-ab))
        la_hi, la_mid, la_lo = _split3(log_a)
        gam_all = _dot(incl_ones, la_hi) + _dot(incl_ones, la_mid) + _dot(incl_ones, la_lo)

        hs = range(heads)
        gam = [gam_all[:, h:h + 1] for h in hs]
        beta = [beta_all[:, B_HEADS + h:B_HEADS + h + 1] for h in hs]
        cols = [slice(h * dk, (h + 1) * dk) for h in hs]
        kh = [k_ref[0, rows, cols[h]] for h in hs]
        qh = [q_ref[0, rows, cols[h]] * (dk ** -0.5) for h in hs]
        vh = [v_ref[0, rows, cols[h]] for h in hs]
        kb = [kh[h].astype(BF16) for h in hs]
        kk = [_dot_nt(kb[h], kb[h]) for h in hs]
        qk = [_dot_nt(qh[h].astype(BF16), kb[h]) for h in hs]

        dec = []
        for h in hs:
            gcol = jnp.broadcast_to(gam[h], (c, c))
            diff = gcol - gcol.T
            dec.append(jnp.where(strict, jnp.exp(jnp.where(strict, diff, 0.0)), 0.0))
        nmat = [beta[h] * kk[h] * dec[h] for h in hs]
        a_qk = [(qk[h] * (dec[h] + eye)).astype(BF16) for h in hs]

        mp = [-(nmat[h] * diag_mask) for h in hs]
        x = [eye + mp[h] for h in hs]
        for _ in range(GDN_INV_BLOCK.bit_length() - 2):
            mpb = [mp[h].astype(BF16) for h in hs]
            mp = [_dot(mpb[h], mpb[h]) for h in hs]
            x = [x[h] + _dot(x[h].astype(BF16), mp[h].astype(BF16)) for h in hs]
        nb = [nmat[h].astype(BF16) for h in hs]
        for lm in level_masks:
            xb = [x[h].astype(BF16) for h in hs]
            cx = [_dot(nb[h] * lm, xb[h]).astype(BF16) for h in hs]
            x = [x[h] - _dot(xb[h], cx[h]) for h in hs]
        xb = [x[h].astype(BF16) for h in hs]

        g_cum = [jnp.exp(gam[h]) for h in hs]
        w = [_dot(xb[h], ((beta[h] * g_cum[h]) * kh[h]).astype(BF16)).astype(BF16) for h in hs]
        u = [_dot(xb[h], (beta[h] * vh[h]).astype(BF16)) for h in hs]
        g_last = [gam[h][c - 1:c, :] for h in hs]
        q_dec = [(qh[h] * g_cum[h]).astype(BF16) for h in hs]
        k_dec = [(kh[h] * jnp.exp(g_last[h] - gam[h])).astype(BF16) for h in hs]

        state = [st_ref[h] for h in hs]
        sb = [state[h].astype(BF16) for h in hs]
        db = [(u[h] - _dot(w[h], sb[h])).astype(BF16) for h in hs]
        out = [_dot(q_dec[h], sb[h]) + _dot(a_qk[h], db[h]) for h in hs]
        for h in hs:
            st_ref[h] = jnp.exp(g_last[h]) * state[h] + _dot_tn(k_dec[h], db[h])
        for h in hs:
            normed = (out[h] * lax.rsqrt(jnp.mean(out[h] * out[h], axis=-1, keepdims=True) + RMS_EPS)
                      * ng_ref[...])
            o_ref[0, rows, cols[h]] = (normed * _silu(z_ref[0, rows, cols[h]])).astype(BF16)
        return carry

    lax.fori_loop(0, n_steps, step, 0)


def _gdn(qkv, z, ab, a_log, dt_bias, norm_g):
    bsz, seq, _ = qkv.shape
    heads = B_HEADS
    gw = heads * B_KEY_DIM
    pad_lanes = lambda v: jnp.pad(v.reshape(1, -1), ((0, 0), (0, LANES - v.size)))
    kern = functools.partial(_gdn_kernel, n_steps=seq // GDN_CHUNK, heads=heads)
    col_block = lambda off: (lambda b: (b, 0, off))
    const2 = lambda b: (0, 0)
    return pl.pallas_call(
        kern,
        grid=(bsz,),
        in_specs=[pl.BlockSpec((1, seq, gw), col_block(0)),
                  pl.BlockSpec((1, seq, gw), col_block(1)),
                  pl.BlockSpec((1, seq, gw), col_block(2)),
                  pl.BlockSpec((1, seq, gw), col_block(0)),
                  pl.BlockSpec((1, seq, LANES), lambda b: (b, 0, 0)),
                  pl.BlockSpec((1, LANES), const2),
                  pl.BlockSpec((1, LANES), const2),
                  pl.BlockSpec((1, B_VAL_DIM), const2)],
        out_specs=pl.BlockSpec((1, seq, gw), col_block(0)),
        out_shape=jax.ShapeDtypeStruct((bsz, seq, B_HEADS * B_VAL_DIM), BF16),
        scratch_shapes=[pltpu.VMEM((heads, B_KEY_DIM, B_VAL_DIM), F32)],
        compiler_params=_params("arbitrary"),
        name="gdn",
    )(qkv, qkv, qkv, z, ab, pad_lanes(a_log), pad_lanes(dt_bias), norm_g.reshape(1, B_VAL_DIM))


def _outproj_kernel(x_ref, mod_ref, attn_ref, dn_ref, wo_ref, g_ref, b_ref, o_ref, *, n_out_chunks):
    gate = mod_ref[0, 5:6, :]
    rows_per = x_ref.shape[1] // n_out_chunks
    for r in range(n_out_chunks):
        rows = slice(r * rows_per, (r + 1) * rows_per)
        mixed = jnp.concatenate([attn_ref[0, rows, :], dn_ref[0, rows, :]], axis=1)
        y = DEEPNORM_ALPHA * x_ref[0, rows, :] + gate * _dot(mixed, wo_ref[...])
        o_ref[0, rows, :] = _layer_norm(y, g_ref[...], b_ref[...])


def _outproj(x, mod, attn, dn, w_out, ln_g, ln_b):
    bsz, seq, d = x.shape
    tm = min(OUT_TOKENS, seq)
    tok = lambda b, i: (b, i, 0)
    const2 = lambda b, i: (0, 0)
    kern = functools.partial(_outproj_kernel, n_out_chunks=max(1, tm // FFN_OUT_ROWS))
    return pl.pallas_call(
        kern,
        grid=(bsz, seq // tm),
        in_specs=[pl.BlockSpec((1, tm, d), tok),
                  pl.BlockSpec((1, N_MOD, d), lambda b, i: (b, 0, 0)),
                  pl.BlockSpec((1, tm, attn.shape[2]), tok),
                  pl.BlockSpec((1, tm, dn.shape[2]), tok),
                  pl.BlockSpec(w_out.shape, const2),
                  pl.BlockSpec((1, d), const2),
                  pl.BlockSpec((1, d), const2)],
        out_specs=pl.BlockSpec((1, tm, d), tok),
        out_shape=jax.ShapeDtypeStruct((bsz, seq, d), F32),
        compiler_params=_params("arbitrary", "arbitrary"),
        name="outproj",
    )(x, mod, attn, dn, w_out.astype(BF16), ln_g.reshape(1, d), ln_b.reshape(1, d))


def kernel(x, c, positions, w_ada, b_ada, ffn1_w1, ffn1_w3, ffn1_w2, ln1_g, ln1_b, w_in, conv_w, a_log, dt_bias, dn_norm_g, w_out, ln2_g, ln2_b, ffn2_w1, ffn2_w3, ffn2_w2, ln3_g, ln3_b):
    bsz, seq, d = x.shape
    pos_f = positions.astype(F32).reshape(bsz, 1, seq)
    for layer in range(w_ada.shape[0]):
        mod = _adaln(c, w_ada[layer], b_ada[layer]).reshape(bsz, N_MOD, d)
        x = _ffn_block(x, mod, ffn1_w1[layer], ffn1_w3[layer], ffn1_w2[layer], ln1_g[layer], ln1_b[layer],
                       mod_base=0)
        qt, qit, wt, k, ki, vt, qkv, z, ab = _proj(x, mod, pos_f, w_in[layer], conv_w[layer])
        attn = _dsa(qt, qit, wt, k, ki, vt)
        dn = _gdn(qkv, z, ab, a_log[layer], dt_bias[layer], dn_norm_g[layer])
        x = _outproj(x, mod, attn, dn, w_out[layer], ln2_g[layer], ln2_b[layer])
        x = _ffn_block(x, mod, ffn2_w1[layer], ffn2_w3[layer], ffn2_w2[layer], ln3_g[layer], ln3_b[layer],
                       mod_base=6)
    return x
```

```python
import functools

import jax
import jax.numpy as jnp
from jax import lax
from jax.experimental import pallas as pl
from jax.experimental.pallas import tpu as pltpu

F32 = jnp.float32
BF16 = jnp.bfloat16

A_HEADS = 8
A_HEAD_DIM = 64
IDX_HEADS = 8
IDX_DIM = 64
TOPK_MAX = 256
ROPE_THETA = 10000.0
B_HEADS = 4
B_KEY_DIM = 128
B_VAL_DIM = 128
CONV_WIDTH = 4
MACARON_WEIGHT = 0.5
DEPTH = 1
DEEPNORM_ALPHA = (2.0 * DEPTH) ** 0.25
LN_EPS = 1e-5
RMS_EPS = 1e-6
N_MOD = 9

LANES = 128
SUBLANES = 8
BF16_ROWS = 16
VMEM_LIMIT_BYTES = 56 * 1024 * 1024

FFN_TOKENS = 1024
FFN_CHUNK = 256
FFN_OUT_ROWS = 512
PROJ_TOKENS = 512
PROJ_CONV_COLS = 256
PROJ_CONV_ROWS = 128
OUT_TOKENS = 1024
ROW_CHUNK = 256
Q_BLOCK = 256
KEY_CHUNK = 256
GDN_CHUNK = 256
GDN_HEADS_PER_STEP = 2
GDN_INV_BLOCK = 16

COUNT_ROWS = 32

NEG_INF = float("-inf")


def _silu(x):
    return x * (1.0 / (1.0 + jnp.exp(-x)))


def _dot(a, b):
    return jnp.dot(a, b, preferred_element_type=F32)


def _dot_nt(a, b):
    return lax.dot_general(a, b, (((1,), (1,)), ((), ())), preferred_element_type=F32)


def _dot_tn(a, b):
    return lax.dot_general(a, b, (((0,), (0,)), ((), ())), preferred_element_type=F32)


def _layer_norm(y, g, b):
    mu = jnp.mean(y, axis=-1, keepdims=True)
    d = y - mu
    var = jnp.mean(d * d, axis=-1, keepdims=True)
    return d * lax.rsqrt(var + LN_EPS) * g + b


def _params(*sem):
    return pltpu.CompilerParams(dimension_semantics=sem, vmem_limit_bytes=VMEM_LIMIT_BYTES)


def _adaln_kernel(c_ref, w_ref, b_ref, o_ref):
    sc = _silu(c_ref[...]).astype(BF16)
    o_ref[...] = _dot(sc, w_ref[...].astype(BF16)) + b_ref[...]


def _adaln(c, w, b):
    bsz, d = c.shape
    n = w.shape[1]
    tn = n // 8
    return pl.pallas_call(
        _adaln_kernel,
        grid=(n // tn,),
        in_specs=[pl.BlockSpec((bsz, d), lambda j: (0, 0)),
                  pl.BlockSpec((d, tn), lambda j: (0, j)),
                  pl.BlockSpec((1, tn), lambda j: (0, j))],
        out_specs=pl.BlockSpec((bsz, tn), lambda j: (0, j)),
        out_shape=jax.ShapeDtypeStruct((bsz, n), F32),
        compiler_params=_params("arbitrary"),
        name="adaln",
    )(c, w, b.reshape(1, n))


def _ffn_kernel(x_ref, mod_ref, w1_ref, w3_ref, w2_ref, g_ref, b_ref, o_ref, u_ref, hid_ref,
                *, mod_base, n_chunks, n_row_chunks, n_out_chunks):
    shift = mod_ref[0, mod_base:mod_base + 1, :]
    scale = mod_ref[0, mod_base + 1:mod_base + 2, :]
    gate = mod_ref[0, mod_base + 2:mod_base + 3, :]

    def prologue(r, carry):
        rows = pl.ds(pl.multiple_of(r * ROW_CHUNK, ROW_CHUNK), ROW_CHUNK)
        u_ref[rows, :] = (x_ref[0, rows, :] * (1.0 + scale) + shift).astype(BF16)
        return carry

    lax.fori_loop(0, n_row_chunks, prologue, 0)

    u = u_ref[...]
    for f in range(n_chunks):
        cols = slice(f * FFN_CHUNK, (f + 1) * FFN_CHUNK)
        a = _dot(u, w1_ref[:, cols])
        b = _dot(u, w3_ref[:, cols])
        hid_ref[:, cols] = (_silu(a) * b).astype(BF16)

    rows_per = x_ref.shape[1] // n_out_chunks
    for r in range(n_out_chunks):
        rows = slice(r * rows_per, (r + 1) * rows_per)
        h = _dot(hid_ref[rows, :], w2_ref[...])
        y =DEEPNORM_ALPHA * x_ref[0, rows, :] + (MACARON_WEIGHT * gate) * h
        o_ref[0, rows, :] = _layer_norm(y, g_ref[...], b_ref[...])


def _ffn_block(x, mod, w1, w3, w2, ln_g, ln_b, *, mod_base):
    bsz, seq, d = x.shape
    dff = w1.shape[1]
    n_chunks = dff // FFN_CHUNK
    tm = min(FFN_TOKENS, seq)
    kern = functools.partial(_ffn_kernel, mod_base=mod_base, n_chunks=n_chunks,
                             n_row_chunks=tm // ROW_CHUNK, n_out_chunks=max(1, tm // FFN_OUT_ROWS))
    const2 = lambda b, i: (0, 0)
    return pl.pallas_call(
        kern,
        grid=(bsz, seq // tm),
        in_specs=[pl.BlockSpec((1, tm, d), lambda b, i: (b, i, 0)),
                  pl.BlockSpec((1, N_MOD, d), lambda b, i: (b, 0, 0)),
                  pl.BlockSpec((d, dff), const2, pipeline_mode=pl.Buffered(1)),
                  pl.BlockSpec((d, dff), const2, pipeline_mode=pl.Buffered(1)),
                  pl.BlockSpec((dff, d), const2, pipeline_mode=pl.Buffered(1)),
                  pl.BlockSpec((1, d), lambda b, i: (0, 0)),
                  pl.BlockSpec((1, d), lambda b, i: (0, 0))],
        out_specs=pl.BlockSpec((1, tm, d), lambda b, i: (b, i, 0)),
        out_shape=jax.ShapeDtypeStruct((bsz, seq, d), F32),
        scratch_shapes=[pltpu.VMEM((tm, d), BF16), pltpu.VMEM((tm, dff), BF16)],
        compiler_params=_params("arbitrary", "arbitrary"),
        name="ffn",
    )(x, mod, w1.astype(BF16), w3.astype(BF16), w2.astype(BF16), ln_g.reshape(1, d), ln_b.reshape(1, d))


_ROW_Q = 0
_ROW_QI = _ROW_Q + A_HEADS * A_HEAD_DIM
_ROW_K = _ROW_QI + IDX_HEADS * IDX_DIM
_ROW_KI = _ROW_K + A_HEAD_DIM
_ROW_V = _ROW_KI + IDX_DIM
_ROW_W = _ROW_V + A_HEAD_DIM
_ROWS_A = _ROW_W + IDX_HEADS
_ROWS_A_PADDED = -(-_ROWS_A // 16) * 16
_COL_QKV = 0
_COL_Z = 2 * B_HEADS * B_KEY_DIM + B_HEADS * B_VAL_DIM
_COL_AB = _COL_Z + B_HEADS * B_VAL_DIM
_COLS_B = _COL_AB + LANES


def _proj_kernel(x_ref, mod_ref, pos_ref, invf_ref, wat_ref, wb_ref, cw_ref,
                 qt_ref, qit_ref, wt_ref, k_ref, ki_ref, vt_ref, qkv_ref, z_ref, ab_ref,
                 u_ref, pat_ref, prev_ref, *, n_row_chunks):
    shift = mod_ref[0, 3:4, :]
    scale = mod_ref[0, 4:5, :]
    tm = x_ref.shape[1]

    def prologue(r, carry):
        rows = pl.ds(pl.multiple_of(r * ROW_CHUNK, ROW_CHUNK), ROW_CHUNK)
        u_ref[rows, :] = (x_ref[0, rows, :] * (1.0 + scale) + shift).astype(BF16)
        return carry

    lax.fori_loop(0, n_row_chunks, prologue, 0)
    u = u_ref[...]

    halo = SUBLANES

    @pl.when(pl.program_id(1) == 0)
    def _():
        prev_ref[...] = jnp.zeros(prev_ref.shape, F32)

    n_normed = 2 * B_HEADS * B_KEY_DIM
    rb = PROJ_CONV_ROWS
    for c0 in range(_COL_QKV, _COL_Z, PROJ_CONV_COLS):
        cols = slice(c0, c0 + PROJ_CONV_COLS)
        prev = prev_ref[:, cols]
        for r0 in range(0, tm, rb):
            raw = _dot(u_ref[r0:r0 + rb, :], wb_ref[:, cols])
            xw = jnp.concatenate([prev, raw], axis=0)
            prev = raw[rb - halo:, :]
            y = raw * cw_ref[CONV_WIDTH - 1:CONV_WIDTH, cols]
            for j in range(CONV_WIDTH - 1):
                sh = CONV_WIDTH - 1 - j
                y = y + xw[halo - sh:halo - sh + rb] * cw_ref[j:j + 1, cols]
            y = _silu(y)
            for h0 in range(0, PROJ_CONV_COLS, B_KEY_DIM):
                t = y[:, h0:h0 + B_KEY_DIM]
                if c0 + h0 < n_normed:
                    t = t * lax.rsqrt(jnp.sum(t * t, axis=-1, keepdims=True) + RMS_EPS)
                qkv_ref[0, r0:r0 + rb, c0 + h0:c0 + h0 + B_KEY_DIM] = t
        prev_ref[:, cols] = prev
    z_ref[0] = _dot(u, wb_ref[:, _COL_Z:_COL_AB])
    ab_ref[0] = _dot(u, wb_ref[:, _COL_AB:_COLS_B])

    ang = invf_ref[...] * pos_ref[0]
    cos = jnp.cos(ang)
    sin = jnp.sin(ang)
    half = A_HEAD_DIM // 2

    pat_ref[...] = _dot_nt(wat_ref[...], u)

    def roped(row0):
        t1 = pat_ref[row0:row0 + half, :]
        t2 = pat_ref[row0 + half:row0 + A_HEAD_DIM, :]
        return t1 * cos - t2 * sin, t2 * cos + t1 * sin

    def store_heads(dst_ref, row_base, scale):
        for h in range(A_HEADS):
            o1, o2 = roped(row_base + h * A_HEAD_DIM)
            o1 = (o1 * scale).astype(BF16)
            o2 = (o2 * scale).astype(BF16)
            lanes = slice(h * Q_BLOCK, (h + 1) * Q_BLOCK)
            for j in range(dst_ref.shape[1]):
                toks = slice(j * Q_BLOCK, (j + 1) * Q_BLOCK)
                dst_ref[0, j, :half, lanes] = o1[:, toks]
                dst_ref[0, j, half:, lanes] = o2[:, toks]

    store_heads(qt_ref, _ROW_Q, A_HEAD_DIM ** -0.5)
    store_heads(qit_ref, _ROW_QI, 1.0)

    k1, k2 = roped(_ROW_K)
    ki1, ki2 = roped(_ROW_KI)
    kk = jnp.concatenate([k1, k2, ki1, ki2], axis=0).T
    k_ref[0] = kk[:, :A_HEAD_DIM].astype(BF16)
    ki_ref[0] = kk[:, A_HEAD_DIM:].astype(BF16)

    for j in range(vt_ref.shape[1]):
        vt_ref[0, j] = pat_ref[_ROW_V:_ROW_V + A_HEAD_DIM, j * KEY_CHUNK:(j + 1) * KEY_CHUNK].astype(BF16)

    wt_ref[0] = pat_ref[_ROW_W:_ROW_W + IDX_HEADS, :] * (IDX_HEADS ** -0.5 * IDX_DIM ** -0.5)


def _proj(x, mod, pos_f, w_in, conv_w):
    bsz, seq, d = x.shape
    tm = min(PROJ_TOKENS, seq)
    splits = (A_HEADS * A_HEAD_DIM, A_HEAD_DIM, A_HEAD_DIM, IDX_HEADS * IDX_DIM, IDX_DIM, IDX_HEADS,
              B_HEADS * B_KEY_DIM, B_HEADS * B_KEY_DIM, B_HEADS * B_VAL_DIM, B_HEADS * B_VAL_DIM,
              B_HEADS, B_HEADS)
    offs = [0]
    for s in splits:
        offs.append(offs[-1] + s)
    cols = [w_in[:, offs[i]:offs[i + 1]] for i in range(len(splits))]
    a_q, a_k, a_v, i_q, i_k, i_w, b_q, b_k, b_v, b_z, b_a, b_b = cols
    wat = jnp.concatenate([a_q, i_q, a_k, i_k, a_v, i_w], axis=1).T.astype(BF16)
    wat = jnp.pad(wat, ((0, _ROWS_A_PADDED - _ROWS_A), (0, 0)))
    wb = jnp.concatenate([b_q, b_k, b_v, b_z, b_a, b_b], axis=1).astype(BF16)
    wb = jnp.pad(wb, ((0, 0), (0, _COLS_B - wb.shape[1])))
    half = A_HEAD_DIM // 2
    inv_freq = (ROPE_THETA ** (-jnp.arange(0, A_HEAD_DIM, 2, dtype=F32) / A_HEAD_DIM)).reshape(half, 1)

    n_kc = tm // KEY_CHUNK
    qkv_w = _COL_Z
    z_w = _COL_AB - _COL_Z
    tok = lambda b, i: (b, i, 0)
    tok_t = lambda b, i: (b, 0, i)
    const2 = lambda b, i: (0, 0)
    n_qb = tm // Q_BLOCK
    q_spec = pl.BlockSpec((1, n_qb, A_HEAD_DIM, A_HEADS * Q_BLOCK), lambda b, i: (b, i, 0, 0))
    out_shapes = (
        jax.ShapeDtypeStruct((bsz, seq // Q_BLOCK, A_HEAD_DIM, A_HEADS * Q_BLOCK), BF16),
        jax.ShapeDtypeStruct((bsz, seq // Q_BLOCK, IDX_DIM, IDX_HEADS * Q_BLOCK), BF16),
        jax.ShapeDtypeStruct((bsz, IDX_HEADS, seq), F32),
        jax.ShapeDtypeStruct((bsz, seq, A_HEAD_DIM), BF16),
        jax.ShapeDtypeStruct((bsz, seq, IDX_DIM), BF16),
        jax.ShapeDtypeStruct((bsz, seq // KEY_CHUNK, A_HEAD_DIM, KEY_CHUNK), BF16),
        jax.ShapeDtypeStruct((bsz, seq, qkv_w), F32),
        jax.ShapeDtypeStruct((bsz, seq, z_w), F32),
        jax.ShapeDtypeStruct((bsz, seq, LANES), F32),
    )
    out_specs = (
        q_spec,
        q_spec,
        pl.BlockSpec((1, IDX_HEADS, tm), tok_t),
        pl.BlockSpec((1, tm, A_HEAD_DIM), tok),
        pl.BlockSpec((1, tm, IDX_DIM), tok),
        pl.BlockSpec((1, n_kc, A_HEAD_DIM, KEY_CHUNK), lambda b, i: (b, i, 0, 0)),
        pl.BlockSpec((1, tm, qkv_w), tok),
        pl.BlockSpec((1, tm, z_w), tok),
        pl.BlockSpec((1, tm, LANES), tok),
    )
    kern = functools.partial(_proj_kernel, n_row_chunks=tm // ROW_CHUNK)
    return pl.pallas_call(
        kern,
        grid=(bsz, seq // tm),
        in_specs=[pl.BlockSpec((1, tm, d), tok),
                  pl.BlockSpec((1, N_MOD, d), lambda b, i: (b, 0, 0)),
                  pl.BlockSpec((1, 1, tm), tok_t),
                  pl.BlockSpec((half, 1), const2),
                  pl.BlockSpec((_ROWS_A_PADDED, d), const2),
                  pl.BlockSpec((d, _COLS_B), const2),
                  pl.BlockSpec(conv_w.shape, const2)],
        out_specs=out_specs,
        out_shape=out_shapes,
        scratch_shapes=[pltpu.VMEM((tm, d), BF16), pltpu.VMEM((_ROWS_A_PADDED, tm), F32),
                        pltpu.VMEM((SUBLANES, qkv_w), F32)],
        compiler_params=_params("arbitrary", "arbitrary"),
        name="proj",
    )(x, mod, pos_f, inv_freq, wat, wb, conv_w)


_KEY_LO = -2139095041
_KEY_HI = 2139095041


def _key_to_float(key):
    bits = jnp.where(key < 0, key ^ 0x7FFFFFFF, key)
    return lax.bitcast_convert_type(bits, F32)


def _fold_rows(t):
    return t.reshape(t.shape[0] // SUBLANES, SUBLANES, t.shape[1])


def _dsa_kernel(qt_ref, qit_ref, wt_ref, k_ref, ki_ref, vt_ref, o_ref, s_ref, lg_ref, ot_ref,
                *, n_sel, search_steps):
    blk = pl.program_id(1)
    kc = KEY_CHUNK
    q_base = blk * Q_BLOCK
    n_keys = q_base + Q_BLOCK
    n_kc = (n_keys + kc - 1) // kc
    row = lax.broadcasted_iota(jnp.int32, (kc, Q_BLOCK), 0)
    col = lax.broadcasted_iota(jnp.int32, (kc, Q_BLOCK), 1)
    wt = wt_ref[0]
    heads = range(A_HEADS)
    lanes = [slice(h * Q_BLOCK, (h + 1) * Q_BLOCK) for h in heads]

    def chunk_rows(c):
        return pl.ds(pl.multiple_of(c * kc, kc), kc)

    def score_chunk(c, carry):
        x = _dot(ki_ref[0, chunk_rows(c), :], qit_ref[0, 0])
        acc = wt[0:1, :] * jnp.maximum(x[:, lanes[0]], 0.0)
        for h in range(1, IDX_HEADS):
            acc = acc + wt[h:h + 1, :] * jnp.maximum(x[:, lanes[h]], 0.0)
        causal = row + (c * kc - q_base) <= col
        s_ref[c] = jnp.where(causal, acc, NEG_INF)
        return carry

    lax.fori_loop(0, n_kc, score_chunk, 0)

    def count(pred):
        def body(c, acc):
            ind = jnp.where(pred(s_ref[c]), 1.0, 0.0)
            return acc + jnp.sum(ind.reshape(kc // COUNT_ROWS, COUNT_ROWS, Q_BLOCK), axis=0)

        acc = lax.fori_loop(0, n_kc, body, jnp.zeros((COUNT_ROWS, Q_BLOCK), F32))
        return jnp.sum(acc, axis=0, keepdims=True)

    def rewrite_scores(fn):
        def body(c, carry):
            s_ref[c] = fn(s_ref[c])
            return carry
        lax.fori_loop(0, n_kc, body, 0)

    @pl.when(n_keys <= n_sel)
    def _():
        rewrite_scores(lambda s: jnp.where(s > NEG_INF, 0.0, NEG_INF))

    @pl.when(n_keys > n_sel)
    def _():
        def bisect(i, carry):
            lo, hi, cnt_lo = carry
            mid = (lo & hi) + ((lo ^ hi) >> 1)
            thr = _key_to_float(mid)
            cnt = count(lambda s: s >= thr)
            ok = cnt >= n_sel
            return jnp.where(ok, mid, lo), jnp.where(ok, hi, mid), jnp.where(ok, cnt, cnt_lo)

        lo0 = jnp.full((1, Q_BLOCK), _KEY_LO, jnp.int32)
        hi0 = jnp.full((1, Q_BLOCK), _KEY_HI, jnp.int32)
        cnt0 = jnp.full((1, Q_BLOCK), float(n_sel), F32)
        lo, _, cnt_lo = lax.fori_loop(0, search_steps, bisect, (lo0, hi0, cnt0))
        thr = _key_to_float(lo)
        has_ties = jnp.max(cnt_lo) > n_sel

        @pl.when(jnp.logical_not(has_ties))
        def _():
            rewrite_scores(lambda s: jnp.where(s >= thr, jnp.where(s > NEG_INF, 0.0, NEG_INF), NEG_INF))

        @pl.when(has_ties)
        def _():
            quota = n_sel - count(lambda s: s > thr)
            lower = (lax.broadcasted_iota(jnp.int32, (kc, kc), 1)
                     < lax.broadcasted_iota(jnp.int32, (kc, kc), 0))
            lower = jnp.where(lower, 1.0, 0.0).astype(BF16)

            def body(c, seen):
                s = s_ref[c]
                eq = jnp.where(s == thr, 1.0, 0.0)
                rank = _dot(lower, eq.astype(BF16)) + seen
                take = jnp.where(s > thr, 1.0, jnp.where(rank < quota, eq, 0.0))
                s_ref[c] = jnp.where(take > 0.5, jnp.where(s > NEG_INF, 0.0, NEG_INF), NEG_INF)
                return seen + jnp.sum(eq, axis=0, keepdims=True)

            lax.fori_loop(0, n_kc, body, jnp.zeros((1, Q_BLOCK), F32))

    def logits_chunk(c, m):
        bias = s_ref[c]
        lg = _dot(k_ref[0, chunk_rows(c), :], qt_ref[0, 0]) + jnp.concatenate([bias] * A_HEADS, axis=1)
        lg_ref[c] = lg
        return jnp.maximum(m, jnp.max(_fold_rows(lg), axis=0))

    m = lax.fori_loop(0, n_kc, logits_chunk, jnp.full((SUBLANES, A_HEADS * Q_BLOCK), NEG_INF, F32))
    m = jnp.max(m, axis=0, keepdims=True)

    ot_ref[...] = jnp.zeros(ot_ref.shape, F32)
    ones_rows = jnp.ones((ot_ref.shape[0] - A_HEAD_DIM, 2 * kc), BF16)

    @pl.when(n_kc % 2 == 1)
    def _():
        lg_ref[n_kc] = jnp.full(lg_ref.shape[1:], NEG_INF, F32)

    def pv_pair(i, carry):
        c0 = 2 * i
        lg = jnp.concatenate([lg_ref[c0], lg_ref[c0 + 1]], axis=0)
        p = jnp.exp((lg - m).astype(BF16))
        vt1 = jnp.where(c0 + 1 < n_kc, vt_ref[0, jnp.minimum(c0 + 1, vt_ref.shape[1] - 1)],
                        jnp.zeros((A_HEAD_DIM, kc), BF16))
        vt_aug = jnp.concatenate([jnp.concatenate([vt_ref[0, c0], vt1], axis=1), ones_rows], axis=0)
        ot_ref[...] += _dot(vt_aug, p)
        return carry

    lax.fori_loop(0, (n_kc + 1) // 2, pv_pair, 0)
    o = ot_ref[:A_HEAD_DIM, :] / ot_ref[A_HEAD_DIM:A_HEAD_DIM + 1, :]
    o = jnp.concatenate([o[:, lanes[h]] for h in heads], axis=0)
    o_ref[0] = o.T.astype(BF16)


def _dsa(qt, qit, wt, k, ki, vt):
    bsz, n_blk, _, hq = qt.shape
    seq = n_blk * Q_BLOCK
    n_sel = min(TOPK_MAX, seq // 4)
    n_kc = seq // KEY_CHUNK
    hd = A_HEADS * A_HEAD_DIM
    kern = functools.partial(_dsa_kernel, n_sel=n_sel, search_steps=32)
    return pl.pallas_call(
        kern,
        grid=(bsz, n_blk),
        in_specs=[pl.BlockSpec((1, 1, A_HEAD_DIM, hq), lambda b, i: (b, i, 0, 0)),
                  pl.BlockSpec((1, 1, IDX_DIM, hq), lambda b, i: (b, i, 0, 0)),
                  pl.BlockSpec((1, IDX_HEADS, Q_BLOCK), lambda b, i: (b, 0, i)),
                  pl.BlockSpec((1, seq, A_HEAD_DIM), lambda b, i: (b, 0, 0)),
                  pl.BlockSpec((1, seq, IDX_DIM), lambda b, i: (b, 0, 0)),
                  pl.BlockSpec((1, n_kc, A_HEAD_DIM, KEY_CHUNK), lambda b, i: (b, 0, 0, 0))],
        out_specs=pl.BlockSpec((1, Q_BLOCK, hd), lambda b, i: (b, i, 0)),
        out_shape=jax.ShapeDtypeStruct((bsz, seq, hd), BF16),
        scratch_shapes=[pltpu.VMEM((n_kc, KEY_CHUNK, Q_BLOCK), F32),
                        pltpu.VMEM((n_kc + n_kc % 2, KEY_CHUNK, hq), F32),
                        pltpu.VMEM((A_HEAD_DIM + BF16_ROWS, hq), F32)],
        compiler_params=_params("arbitrary", "arbitrary"),
        name="dsa",
    )(qt, qit, wt, k, ki, vt)


def _softplus(x):
    return jnp.maximum(x, 0.0) + jnp.log(1.0 + jnp.exp(-jnp.abs(x)))


def _split3(x):
    hi = x.astype(BF16)
    r = x - hi.astype(F32)
    mid = r.astype(BF16)
    lo = (r - mid.astype(F32)).astype(BF16)
    return hi, mid, lo


def _gdn_kernel(q_ref, k_ref, v_ref, z_ref, ab_ref, alog_ref, dtb_ref, ng_ref,
                o_ref, st_ref, *, n_steps, heads):
    c = GDN_CHUNK
    dk = B_KEY_DIM

    st_ref[...] = jnp.zeros(st_ref.shape, F32)
    ri = lax.broadcasted_iota(jnp.int32, (c, c), 0)
    ci = lax.broadcasted_iota(jnp.int32, (c, c), 1)
    strict = ri > ci
    eye = jnp.where(ri == ci, 1.0, 0.0)
    incl_ones = jnp.where(ri >= ci, 1.0, 0.0).astype(BF16)
    inv_shift = GDN_INV_BLOCK.bit_length() - 1
    diag_mask = jnp.where((ri >> inv_shift) == (ci >> inv_shift), 1.0, 0.0)
    level_masks = []
    for ls in range(inv_shift, c.bit_length() - 1):
        lower_left = jnp.where((ri >> ls) == (ci >> ls) + 1, 1.0, 0.0) * jnp.where(((ci >> ls) & 1) == 0, 1.0, 0.0)
        level_masks.append(lower_left.astype(BF16))

    def step(s, carry):
        rows = pl.ds(pl.multiple_of(s * c, c), c)
        ab = ab_ref[0, rows, :]
        log_a = -jnp.exp(alog_ref[...]) * _softplus(ab + dtb_ref[...])
        beta_all = 1.0 / (1.0 + jnp.exp(-ab))
        la_hi, la_mid, la_lo = _split3(log_a)
        gam_all = _dot(incl_ones, la_hi) + _dot(incl_ones, la_mid) + _dot(incl_ones, la_lo)

        hs = range(heads)
        gam = [gam_all[:, h:h + 1] for h in hs]
        beta = [beta_all[:, B_HEADS + h:B_HEADS + h + 1] for h in hs]
        cols = [slice(h * dk, (h + 1) * dk) for h in hs]
        kh = [k_ref[0, rows, cols[h]] for h in hs]
        qh = [q_ref[0, rows, cols[h]] * (dk ** -0.5) for h in hs]
        vh = [v_ref[0, rows, cols[h]] for h in hs]
        kb = [kh[h].astype(BF16) for h in hs]
        kk = [_dot_nt(kb[h], kb[h]) for h in hs]
        qk = [_dot_nt(qh[h].astype(BF16), kb[h]) for h in hs]

        dec = []
        for h in hs:
            gcol = jnp.broadcast_to(gam[h], (c, c))
            diff = gcol - gcol.T
            dec.append(jnp.where(strict, jnp.exp(jnp.where(strict, diff, 0.0)), 0.0))
        nmat = [beta[h] * kk[h] * dec[h] for h in hs]
        a_qk = [(qk[h] * (dec[h] + eye)).astype(BF16) for h in hs]

        mp = [-(nmat[h] * diag_mask) for h in hs]
        x = [eye + mp[h] for h in hs]
        for _ in range(GDN_INV_BLOCK.bit_length() - 2):
            mpb = [mp[h].astype(BF16) for h in hs]
            mp = [_dot(mpb[h], mpb[h]) for h in hs]
            x = [x[h] + _dot(x[h].astype(BF16), mp[h].astype(BF16)) for h in hs]
        nb = [nmat[h].astype(BF16) for h in hs]
        for lm in level_masks:
            xb = [x[h].astype(BF16) for h in hs]
            cx = [_dot(nb[h] * lm, xb[h]).astype(BF16) for h in hs]
            x = [x[h] - _dot(xb[h], cx[h]) for h in hs]
        xb = [x[h].astype(BF16) for h in hs]

        g_cum = [jnp.exp(gam[h]) for h in hs]
        w = [_dot(xb[h], ((beta[h] * g_cum[h]) * kh[h]).astype(BF16)).astype(BF16) for h in hs]
        u = [_dot(xb[h], (beta[h] * vh[h]).astype(BF16)) for h in hs]
        g_last = [gam[h][c - 1:c, :] for h in hs]
        q_dec = [(qh[h] * g_cum[h]).astype(BF16) for h in hs]
        k_dec = [(kh[h] * jnp.exp(g_last[h] - gam[h])).astype(BF16) for h in hs]

        state = [st_ref[h] for h in hs]
        sb = [state[h].astype(BF16) for h in hs]
        db = [(u[h] - _dot(w[h], sb[h])).astype(BF16) for h in hs]
        out = [_dot(q_dec[h], sb[h]) + _dot(a_qk[h], db[h]) for h in hs]
        for h in hs:
            st_ref[h] = jnp.exp(g_last[h]) * state[h] + _dot_tn(k_dec[h], db[h])
        for h in hs:
            normed = (out[h] * lax.rsqrt(jnp.mean(out[h] * out[h], axis=-1, keepdims=True) + RMS_EPS)
                      * ng_ref[...])
            o_ref[0, rows, cols[h]] = (normed * _silu(z_ref[0, rows, cols[h]])).astype(BF16)
        return carry

    lax.fori_loop(0, n_steps, step, 0)


def _gdn(qkv, z, ab, a_log, dt_bias, norm_g):
    bsz, seq, _ = qkv.shape
    heads = B_HEADS
    gw = heads * B_KEY_DIM
    pad_lanes = lambda v: jnp.pad(v.reshape(1, -1), ((0, 0), (0, LANES - v.size)))
    kern = functools.partial(_gdn_kernel, n_steps=seq // GDN_CHUNK, heads=heads)
    col_block = lambda off: (lambda b: (b, 0, off))
    const2 = lambda b: (0, 0)
    return pl.pallas_call(
        kern,
        grid=(bsz,),
        in_specs=[pl.BlockSpec((1, seq, gw), col_block(0)),
                  pl.BlockSpec((1, seq, gw), col_block(1)),
                  pl.BlockSpec((1, seq, gw), col_block(2)),
                  pl.BlockSpec((1, seq, gw), col_block(0)),
                  pl.BlockSpec((1, seq, LANES), lambda b: (b, 0, 0)),
                  pl.BlockSpec((1, LANES), const2),
                  pl.BlockSpec((1, LANES), const2),
                  pl.BlockSpec((1, B_VAL_DIM), const2)],
        out_specs=pl.BlockSpec((1, seq, gw), col_block(0)),
        out_shape=jax.ShapeDtypeStruct((bsz, seq, B_HEADS * B_VAL_DIM), BF16),
        scratch_shapes=[pltpu.VMEM((heads, B_KEY_DIM, B_VAL_DIM), F32)],
        compiler_params=_params("arbitrary"),
        name="gdn",
    )(qkv, qkv, qkv, z, ab, pad_lanes(a_log), pad_lanes(dt_bias), norm_g.reshape(1, B_VAL_DIM))


def _outproj_kernel(x_ref, mod_ref, attn_ref, dn_ref, wo_ref, g_ref, b_ref, o_ref, *, n_out_chunks):
    gate = mod_ref[0, 5:6, :]
    rows_per = x_ref.shape[1] // n_out_chunks
    for r in range(n_out_chunks):
        rows = slice(r * rows_per, (r + 1) * rows_per)
        mixed = jnp.concatenate([attn_ref[0, rows, :], dn_ref[0, rows, :]], axis=1)
        y = DEEPNORM_ALPHA * x_ref[0, rows, :] + gate * _dot(mixed, wo_ref[...])
        o_ref[0, rows, :] = _layer_norm(y, g_ref[...], b_ref[...])


def _outproj(x, mod, attn, dn, w_out, ln_g, ln_b):
    bsz, seq, d = x.shape
    tm = min(OUT_TOKENS, seq)
    tok = lambda b, i: (b, i, 0)
    const2 = lambda b, i: (0, 0)
    kern = functools.partial(_outproj_kernel, n_out_chunks=max(1, tm // FFN_OUT_ROWS))
    return pl.pallas_call(
        kern,
        grid=(bsz, seq // tm),
        in_specs=[pl.BlockSpec((1, tm, d), tok),
                  pl.BlockSpec((1, N_MOD, d), lambda b, i: (b, 0, 0)),
                  pl.BlockSpec((1, tm, attn.shape[2]), tok),
                  pl.BlockSpec((1, tm, dn.shape[2]), tok),
                  pl.BlockSpec(w_out.shape, const2),
                  pl.BlockSpec((1, d), const2),
                  pl.BlockSpec((1, d), const2)],
        out_specs=pl.BlockSpec((1, tm, d), tok),
        out_shape=jax.ShapeDtypeStruct((bsz, seq, d), F32),
        compiler_params=_params("arbitrary", "arbitrary"),
        name="outproj",
    )(x, mod, attn, dn, w_out.astype(BF16), ln_g.reshape(1, d), ln_b.reshape(1, d))


def kernel(x, c, positions, w_ada, b_ada, ffn1_w1, ffn1_w3, ffn1_w2, ln1_g, ln1_b, w_in, conv_w, a_log, dt_bias, dn_norm_g, w_out, ln2_g, ln2_b, ffn2_w1, ffn2_w3, ffn2_w2, ln3_g, ln3_b):
    bsz, seq, d = x.shape
    pos_f = positions.astype(F32).reshape(bsz, 1, seq)
    for layer in range(w_ada.shape[0]):
        mod = _adaln(c, w_ada[layer], b_ada[layer]).reshape(bsz, N_MOD, d)
        x = _ffn_block(x, mod, ffn1_w1[layer], ffn1_w3[layer], ffn1_w2[layer], ln1_g[layer], ln1_b[layer],
                       mod_base=0)
        qt, qit, wt, k, ki, vt, qkv, z, ab = _proj(x, mod, pos_f, w_in[layer], conv_w[layer])
        attn = _dsa(qt, qit, wt, k, ki, vt)
        dn = _gdn(qkv, z, ab, a_log[layer], dt_bias[layer], dn_norm_g[layer])
        x = _outproj(x, mod, attn, dn, w_out[layer], ln2_g[layer], ln2_b[layer])
        x = _ffn_block(x, mod, ffn2_w1[layer], ffn2_w3[layer], ffn2_w2[layer], ln3_g[layer], ln3_b[layer],
                       mod_base=6)
    return x
```

```python
import functools

import jax
import jax.numpy as jnp
from jax import lax
from jax.experimental import pallas as pl
from jax.experimental.pallas import tpu as pltpu

F32 = jnp.float32
BF16 = jnp.bfloat16

A_HEADS = 8
A_HEAD_DIM = 64
IDX_HEADS = 8
IDX_DIM = 64
TOPK_MAX = 256
ROPE_THETA = 10000.0
B_HEADS = 4
B_KEY_DIM = 128
B_VAL_DIM = 128
CONV_WIDTH = 4
MACARON_WEIGHT = 0.5
DEPTH = 1
DEEPNORM_ALPHA = (2.0 * DEPTH) ** 0.25
LN_EPS = 1e-5
RMS_EPS = 1e-6
N_MOD = 9

LANES = 128
SUBLANES = 8
BF16_ROWS = 16
VMEM_LIMIT_BYTES = 56 * 1024 * 1024

FFN_TOKENS = 1024
FFN_CHUNK = 256
FFN_OUT_ROWS = 512
PROJ_TOKENS = 512
PROJ_CONV_COLS = 256
PROJ_CONV_ROWS = 128
OUT_TOKENS = 1024
ROW_CHUNK = 256
Q_BLOCK = 256
KEY_CHUNK = 256
GDN_CHUNK = 256
GDN_HEADS_PER_STEP = 2
GDN_INV_BLOCK = 16

COUNT_ROWS = 32

NEG_INF = float("-inf")


def _silu(x):
    return x * (1.0 / (1.0 + jnp.exp(-x)))


def _dot(a, b):
    return jnp.dot(a, b, preferred_element_type=F32)


def _dot_nt(a, b):
    return lax.dot_general(a, b, (((1,), (1,)), ((), ())), preferred_element_type=F32)


def _dot_tn(a, b):
    return lax.dot_general(a, b, (((0,), (0,)), ((), ())), preferred_element_type=F32)


def _layer_norm(y, g, b):
    mu = jnp.mean(y, axis=-1, keepdims=True)
    d = y - mu
    var = jnp.mean(d * d, axis=-1, keepdims=True)
    return d * lax.rsqrt(var + LN_EPS) * g + b


def _params(*sem):
    return pltpu.CompilerParams(dimension_semantics=sem, vmem_limit_bytes=VMEM_LIMIT_BYTES)


def _adaln_kernel(c_ref, w_ref, b_ref, o_ref):
    sc = _silu(c_ref[...]).astype(BF16)
    o_ref[...] = _dot(sc, w_ref[...].astype(BF16)) + b_ref[...]


def _adaln(c, w, b):
    bsz, d = c.shape
    n = w.shape[1]
    tn = n // 8
    return pl.pallas_call(
        _adaln_kernel,
        grid=(n // tn,),
        in_specs=[pl.BlockSpec((bsz, d), lambda j: (0, 0)),
                  pl.BlockSpec((d, tn), lambda j: (0, j)),
                  pl.BlockSpec((1, tn), lambda j: (0, j))],
        out_specs=pl.BlockSpec((bsz, tn), lambda j: (0, j)),
        out_shape=jax.ShapeDtypeStruct((bsz, n), F32),
        compiler_params=_params("arbitrary"),
        name="adaln",
    )(c, w, b.reshape(1, n))


def _ffn_kernel(x_ref, mod_ref, w1_ref, w3_ref, w2_ref, g_ref, b_ref, o_ref, u_ref, hid_ref,
                *, mod_base, n_chunks, n_row_chunks, n_out_chunks):
    shift = mod_ref[0, mod_base:mod_base + 1, :]
    scale = mod_ref[0, mod_base + 1:mod_base + 2, :]
    gate = mod_ref[0, mod_base + 2:mod_base + 3, :]

    def prologue(r, carry):
        rows = pl.ds(pl.multiple_of(r * ROW_CHUNK, ROW_CHUNK), ROW_CHUNK)
        u_ref[rows, :] = (x_ref[0, rows, :] * (1.0 + scale) + shift).astype(BF16)
        return carry

    lax.fori_loop(0, n_row_chunks, prologue, 0)

    u = u_ref[...]
    for f in range(n_chunks):
        cols = slice(f * FFN_CHUNK, (f + 1) * FFN_CHUNK)
        a = _dot(u, w1_ref[:, cols])
        b = _dot(u, w3_ref[:, cols])
        hid_ref[:, cols] = (_silu(a) * b).astype(BF16)

    rows_per = x_ref.shape[1] // n_out_chunks
    for r in range(n_out_chunks):
        rows = slice(r * rows_per, (r + 1) * rows_per)
        h = _dot(hid_ref[rows, :], w2_ref[...])
        y =DEEPNORM_ALPHA * x_ref[0, rows, :] + (MACARON_WEIGHT * gate) * h
        o_ref[0, rows, :] = _layer_norm(y, g_ref[...], b_ref[...])


def _ffn_block(x, mod, w1, w3, w2, ln_g, ln_b, *, mod_base):
    bsz, seq, d = x.shape
    dff = w1.shape[1]
    n_chunks = dff // FFN_CHUNK
    tm = min(FFN_TOKENS, seq)
    kern = functools.partial(_ffn_kernel, mod_base=mod_base, n_chunks=n_chunks,
                             n_row_chunks=tm // ROW_CHUNK, n_out_chunks=max(1, tm // FFN_OUT_ROWS))
    const2 = lambda b, i: (0, 0)
    return pl.pallas_call(
        kern,
        grid=(bsz, seq // tm),
        in_specs=[pl.BlockSpec((1, tm, d), lambda b, i: (b, i, 0)),
                  pl.BlockSpec((1, N_MOD, d), lambda b, i: (b, 0, 0)),
                  pl.BlockSpec((d, dff), const2, pipeline_mode=pl.Buffered(1)),
                  pl.BlockSpec((d, dff), const2, pipeline_mode=pl.Buffered(1)),
                  pl.BlockSpec((dff, d), const2, pipeline_mode=pl.Buffered(1)),
                  pl.BlockSpec((1, d), lambda b, i: (0, 0)),
                  pl.BlockSpec((1, d), lambda b, i: (0, 0))],
        out_specs=pl.BlockSpec((1, tm, d), lambda b, i: (b, i, 0)),
        out_shape=jax.ShapeDtypeStruct((bsz, seq, d), F32),
        scratch_shapes=[pltpu.VMEM((tm, d), BF16), pltpu.VMEM((tm, dff), BF16)],
        compiler_params=_params("arbitrary", "arbitrary"),
        name="ffn",
    )(x, mod, w1.astype(BF16), w3.astype(BF16), w2.astype(BF16), ln_g.reshape(1, d), ln_b.reshape(1, d))


_ROW_Q = 0
_ROW_QI = _ROW_Q + A_HEADS * A_HEAD_DIM
_ROW_K = _ROW_QI + IDX_HEADS * IDX_DIM
_ROW_KI = _ROW_K + A_HEAD_DIM
_ROW_V = _ROW_KI + IDX_DIM
_ROW_W = _ROW_V + A_HEAD_DIM
_ROWS_A = _ROW_W + IDX_HEADS
_ROWS_A_PADDED = -(-_ROWS_A // 16) * 16
_COL_QKV = 0
_COL_Z = 2 * B_HEADS * B_KEY_DIM + B_HEADS * B_VAL_DIM
_COL_AB = _COL_Z + B_HEADS * B_VAL_DIM
_COLS_B = _COL_AB + LANES


def _proj_kernel(x_ref, mod_ref, pos_ref, invf_ref, wat_ref, wb_ref, cw_ref,
                 qt_ref, qit_ref, wt_ref, k_ref, ki_ref, vt_ref, qkv_ref, z_ref, ab_ref,
                 u_ref, pat_ref, prev_ref, *, n_row_chunks):
    shift = mod_ref[0, 3:4, :]
    scale = mod_ref[0, 4:5, :]
    tm = x_ref.shape[1]

    def prologue(r, carry):
        rows = pl.ds(pl.multiple_of(r * ROW_CHUNK, ROW_CHUNK), ROW_CHUNK)
        u_ref[rows, :] = (x_ref[0, rows, :] * (1.0 + scale) + shift).astype(BF16)
        return carry

    lax.fori_loop(0, n_row_chunks, prologue, 0)
    u = u_ref[...]

    halo = SUBLANES

    @pl.when(pl.program_id(1) == 0)
    def _():
        prev_ref[...] = jnp.zeros(prev_ref.shape, F32)

    n_normed = 2 * B_HEADS * B_KEY_DIM
    rb = PROJ_CONV_ROWS
    for c0 in range(_COL_QKV, _COL_Z, PROJ_CONV_COLS):
        cols = slice(c0, c0 + PROJ_CONV_COLS)
        prev = prev_ref[:, cols]
        for r0 in range(0, tm, rb):
            raw = _dot(u_ref[r0:r0 + rb, :], wb_ref[:, cols])
            xw = jnp.concatenate([prev, raw], axis=0)
            prev = raw[rb - halo:, :]
            y = raw * cw_ref[CONV_WIDTH - 1:CONV_WIDTH, cols]
            for j in range(CONV_WIDTH - 1):
                sh = CONV_WIDTH - 1 - j
                y = y + xw[halo - sh:halo - sh + rb] * cw_ref[j:j + 1, cols]
            y = _silu(y)
            for h0 in range(0, PROJ_CONV_COLS, B_KEY_DIM):
                t = y[:, h0:h0 + B_KEY_DIM]
                if c0 + h0 < n_normed:
                    t = t * lax.rsqrt(jnp.sum(t * t, axis=-1, keepdims=True) + RMS_EPS)
                qkv_ref[0, r0:r0 + rb, c0 + h0:c0 + h0 + B_KEY_DIM] = t
        prev_ref[:, cols] = prev
    z_ref[0] = _dot(u, wb_ref[:, _COL_Z:_COL_AB])
    ab_ref[0] = _dot(u, wb_ref[:, _COL_AB:_COLS_B])

    ang = invf_ref[...] * pos_ref[0]
    cos = jnp.cos(ang)
    sin = jnp.sin(ang)
    half = A_HEAD_DIM // 2

    pat_ref[...] = _dot_nt(wat_ref[...], u)

    def roped(row0):
        t1 = pat_ref[row0:row0 + half, :]
        t2 = pat_ref[row0 + half:row0 + A_HEAD_DIM, :]
        return t1 * cos - t2 * sin, t2 * cos + t1 * sin

    def store_heads(dst_ref, row_base, scale):
        for h in range(A_HEADS):
            o1, o2 = roped(row_base + h * A_HEAD_DIM)
            o1 = (o1 * scale).astype(BF16)
            o2 = (o2 * scale).astype(BF16)
            lanes = slice(h * Q_BLOCK, (h + 1) * Q_BLOCK)
            for j in range(dst_ref.shape[1]):
                toks = slice(j * Q_BLOCK, (j + 1) * Q_BLOCK)
                dst_ref[0, j, :half, lanes] = o1[:, toks]
                dst_ref[0, j, half:, lanes] = o2[:, toks]

    store_heads(qt_ref, _ROW_Q, A_HEAD_DIM ** -0.5)
    store_heads(qit_ref, _ROW_QI, 1.0)

    k1, k2 = roped(_ROW_K)
    ki1, ki2 = roped(_ROW_KI)
    kk = jnp.concatenate([k1, k2, ki1, ki2], axis=0).T
    k_ref[0] = kk[:, :A_HEAD_DIM].astype(BF16)
    ki_ref[0] = kk[:, A_HEAD_DIM:].astype(BF16)

    for j in range(vt_ref.shape[1]):
        vt_ref[0, j] = pat_ref[_ROW_V:_ROW_V + A_HEAD_DIM, j * KEY_CHUNK:(j + 1) * KEY_CHUNK].astype(BF16)

    wt_ref[0] = pat_ref[_ROW_W:_ROW_W + IDX_HEADS, :] * (IDX_HEADS ** -0.5 * IDX_DIM ** -0.5)


def _proj(x, mod, pos_f, w_in, conv_w):
    bsz, seq, d = x.shape
    tm = min(PROJ_TOKENS, seq)
    splits = (A_HEADS * A_HEAD_DIM, A_HEAD_DIM, A_HEAD_DIM, IDX_HEADS * IDX_DIM, IDX_DIM, IDX_HEADS,
              B_HEADS * B_KEY_DIM, B_HEADS * B_KEY_DIM, B_HEADS * B_VAL_DIM, B_HEADS * B_VAL_DIM,
              B_HEADS, B_HEADS)
    offs = [0]
    for s in splits:
        offs.append(offs[-1] + s)
    cols = [w_in[:, offs[i]:offs[i + 1]] for i in range(len(splits))]
    a_q, a_k, a_v, i_q, i_k, i_w, b_q, b_k, b_v, b_z, b_a, b_b = cols
    wat = jnp.concatenate([a_q, i_q, a_k, i_k, a_v, i_w], axis=1).T.astype(BF16)
    wat = jnp.pad(wat, ((0, _ROWS_A_PADDED - _ROWS_A), (0, 0)))
    wb = jnp.concatenate([b_q, b_k, b_v, b_z, b_a, b_b], axis=1).astype(BF16)
    wb = jnp.pad(wb, ((0, 0), (0, _COLS_B - wb.shape[1])))
    half = A_HEAD_DIM // 2
    inv_freq = (ROPE_THETA ** (-jnp.arange(0, A_HEAD_DIM, 2, dtype=F32) / A_HEAD_DIM)).reshape(half, 1)

    n_kc = tm // KEY_CHUNK
    qkv_w = _COL_Z
    z_w = _COL_AB - _COL_Z
    tok = lambda b, i: (b, i, 0)
    tok_t = lambda b, i: (b, 0, i)
    const2 = lambda b, i: (0, 0)
    n_qb = tm // Q_BLOCK
    q_spec = pl.BlockSpec((1, n_qb, A_HEAD_DIM, A_HEADS * Q_BLOCK), lambda b, i: (b, i, 0, 0))
    out_shapes = (
        jax.ShapeDtypeStruct((bsz, seq // Q_BLOCK, A_HEAD_DIM, A_HEADS * Q_BLOCK), BF16),
        jax.ShapeDtypeStruct((bsz, seq // Q_BLOCK, IDX_DIM, IDX_HEADS * Q_BLOCK), BF16),
        jax.ShapeDtypeStruct((bsz, IDX_HEADS, seq), F32),
        jax.ShapeDtypeStruct((bsz, seq, A_HEAD_DIM), BF16),
        jax.ShapeDtypeStruct((bsz, seq, IDX_DIM), BF16),
        jax.ShapeDtypeStruct((bsz, seq // KEY_CHUNK, A_HEAD_DIM, KEY_CHUNK), BF16),
        jax.ShapeDtypeStruct((bsz, seq, qkv_w), F32),
        jax.ShapeDtypeStruct((bsz, seq, z_w), F32),
        jax.ShapeDtypeStruct((bsz, seq, LANES), F32),
    )
    out_specs = (
        q_spec,
        q_spec,
        pl.BlockSpec((1, IDX_HEADS, tm), tok_t),
        pl.BlockSpec((1, tm, A_HEAD_DIM), tok),
        pl.BlockSpec((1, tm, IDX_DIM), tok),
        pl.BlockSpec((1, n_kc, A_HEAD_DIM, KEY_CHUNK), lambda b, i: (b, i, 0, 0)),
        pl.BlockSpec((1, tm, qkv_w), tok),
        pl.BlockSpec((1, tm, z_w), tok),
        pl.BlockSpec((1, tm, LANES), tok),
    )
    kern = functools.partial(_proj_kernel, n_row_chunks=tm // ROW_CHUNK)
    return pl.pallas_call(
        kern,
        grid=(bsz, seq // tm),
        in_specs=[pl.BlockSpec((1, tm, d), tok),
                  pl.BlockSpec((1, N_MOD, d), lambda b, i: (b, 0, 0)),
                  pl.BlockSpec((1, 1, tm), tok_t),
                  pl.BlockSpec((half, 1), const2),
                  pl.BlockSpec((_ROWS_A_PADDED, d), const2),
                  pl.BlockSpec((d, _COLS_B), const2),
                  pl.BlockSpec(conv_w.shape, const2)],
        out_specs=out_specs,
        out_shape=out_shapes,
        scratch_shapes=[pltpu.VMEM((tm, d), BF16), pltpu.VMEM((_ROWS_A_PADDED, tm), F32),
                        pltpu.VMEM((SUBLANES, qkv_w), F32)],
        compiler_params=_params("arbitrary", "arbitrary"),
        name="proj",
    )(x, mod, pos_f, inv_freq, wat, wb, conv_w)


_KEY_LO = -2139095041
_KEY_HI = 2139095041


def _key_to_float(key):
    bits = jnp.where(key < 0, key ^ 0x7FFFFFFF, key)
    return lax.bitcast_convert_type(bits, F32)


def _fold_rows(t):
    return t.reshape(t.shape[0] // SUBLANES, SUBLANES, t.shape[1])


def _dsa_kernel(qt_ref, qit_ref, wt_ref, k_ref, ki_ref, vt_ref, o_ref, s_ref, lg_ref, ot_ref,
                *, n_sel, search_steps):
    blk = pl.program_id(1)
    kc = KEY_CHUNK
    q_base = blk * Q_BLOCK
    n_keys = q_base + Q_BLOCK
    n_kc = (n_keys + kc - 1) // kc
    row = lax.broadcasted_iota(jnp.int32, (kc, Q_BLOCK), 0)
    col = lax.broadcasted_iota(jnp.int32, (kc, Q_BLOCK), 1)
    wt = wt_ref[0]
    heads = range(A_HEADS)
    lanes = [slice(h * Q_BLOCK, (h + 1) * Q_BLOCK) for h in heads]

    def chunk_rows(c):
        if isinstance(c, int):
            return slice(c * kc, (c + 1) * kc)
        return pl.ds(pl.multiple_of(c * kc, kc), kc)

    def per_chunk_count(make_variant):
        return lax.switch(n_kc - 1, [make_variant(n) for n in range(1, s_ref.shape[0] + 1)])

    def score_chunk(c):
        x = _dot(ki_ref[0, chunk_rows(c), :], qit_ref[0, 0])
        acc = wt[0:1, :] * jnp.maximum(x[:, lanes[0]], 0.0)
        for h in range(1, IDX_HEADS):
            acc = acc + wt[h:h + 1, :] * jnp.maximum(x[:, lanes[h]], 0.0)
        causal = row + (c * kc - q_base) <= col
        s_ref[c] = jnp.where(causal, acc, NEG_INF)

    def score_all(n):
        def run():
            for c in range(n):
                score_chunk(c)
            return 0
        return run

    per_chunk_count(score_all)

    def count(pred):
        def body(c, acc):
            ind = jnp.where(pred(s_ref[c]), 1.0, 0.0)
            return acc + jnp.sum(ind.reshape(kc // COUNT_ROWS, COUNT_ROWS, Q_BLOCK), axis=0)

        acc = lax.fori_loop(0, n_kc, body, jnp.zeros((COUNT_ROWS, Q_BLOCK), F32))
        return jnp.sum(acc, axis=0, keepdims=True)

    def rewrite_scores(fn):
        def body(c, carry):
            s_ref[c] = fn(s_ref[c])
            return carry
        lax.fori_loop(0, n_kc, body, 0)

    @pl.when(n_keys <= n_sel)
    def _():
        rewrite_scores(lambda s: jnp.where(s > NEG_INF, 0.0, NEG_INF))

    @pl.when(n_keys > n_sel)
    def _():
        def bisect(i, carry):
            lo, hi, cnt_lo = carry
            mid = (lo & hi) + ((lo ^ hi) >> 1)
            thr = _key_to_float(mid)
            cnt = count(lambda s: s >= thr)
            ok = cnt >= n_sel
            return jnp.where(ok, mid, lo), jnp.where(ok, hi, mid), jnp.where(ok, cnt, cnt_lo)

        lo0 = jnp.full((1, Q_BLOCK), _KEY_LO, jnp.int32)
        hi0 = jnp.full((1, Q_BLOCK), _KEY_HI, jnp.int32)
        cnt0 = jnp.full((1, Q_BLOCK), float(n_sel), F32)

        def search_over(n):
            def count_ge(thr):
                acc = jnp.zeros((COUNT_ROWS, Q_BLOCK), F32)
                for c in range(n):
                    ind = jnp.where(s_ref[c] >= thr, 1.0, 0.0)
                    acc = acc + jnp.sum(ind.reshape(kc // COUNT_ROWS, COUNT_ROWS, Q_BLOCK), axis=0)
                return jnp.sum(acc, axis=0, keepdims=True)

            def step(i, carry):
                lo, hi, cnt_lo = carry
                mid = (lo & hi) + ((lo ^ hi) >> 1)
                cnt = count_ge(_key_to_float(mid))
                ok = cnt >= n_sel
                return jnp.where(ok, mid, lo), jnp.where(ok, hi, mid), jnp.where(ok, cnt, cnt_lo)

            return lambda: lax.fori_loop(0, search_steps, step, (lo0, hi0, cnt0))

        lo, _, cnt_lo = lax.switch(n_kc - 1, [search_over(n) for n in range(1, s_ref.shape[0] + 1)])
        thr = _key_to_float(lo)
        has_ties = jnp.max(cnt_lo) > n_sel

        @pl.when(jnp.logical_not(has_ties))
        def _():
            rewrite_scores(lambda s: jnp.where(s >= thr, jnp.where(s > NEG_INF, 0.0, NEG_INF), NEG_INF))

        @pl.when(has_ties)
        def _():
            quota = n_sel - count(lambda s: s > thr)
            lower = (lax.broadcasted_iota(jnp.int32, (kc, kc), 1)
                     < lax.broadcasted_iota(jnp.int32, (kc, kc), 0))
            lower = jnp.where(lower, 1.0, 0.0).astype(BF16)

            def body(c, seen):
                s = s_ref[c]
                eq = jnp.where(s == thr, 1.0, 0.0)
                rank = _dot(lower, eq.astype(BF16)) + seen
                take = jnp.where(s > thr, 1.0, jnp.where(rank < quota, eq, 0.0))
                s_ref[c] = jnp.where(take > 0.5, jnp.where(s > NEG_INF, 0.0, NEG_INF), NEG_INF)
                return seen + jnp.sum(eq, axis=0, keepdims=True)

            lax.fori_loop(0, n_kc, body, jnp.zeros((1, Q_BLOCK), F32))

    def logits_chunk(c, m):
        bias = s_ref[c]
        lg = _dot(k_ref[0, chunk_rows(c), :], qt_ref[0, 0]) + jnp.concatenate([bias] * A_HEADS, axis=1)
        lg_ref[c] = lg
        return jnp.maximum(m, jnp.max(_fold_rows(lg), axis=0))

    def logits_all(n):
        def run():
            m = jnp.full((SUBLANES, A_HEADS * Q_BLOCK), NEG_INF, F32)
            for c in range(n):
                m = logits_chunk(c, m)
            return m
        return run

    m = jnp.max(per_chunk_count(logits_all), axis=0, keepdims=True)

    n_ones = ot_ref.shape[0] - A_HEAD_DIM

    def pv_all(n):
        def run():
            acc = None
            for c0 in range(0, n, 2):
                cs = range(c0, min(c0 + 2, n))
                lg = jnp.concatenate([lg_ref[c] for c in cs], axis=0)
                p = jnp.exp((lg - m).astype(BF16))
                vt_aug = jnp.concatenate([jnp.concatenate([vt_ref[0, c] for c in cs], axis=1),
                                          jnp.ones((n_ones, len(cs) * kc), BF16)], axis=0)
                part = _dot(vt_aug, p)
                acc = part if acc is None else acc + part
            ot_ref[...] = acc
            return 0
        return run

    per_chunk_count(pv_all)
    o = ot_ref[:A_HEAD_DIM, :] / ot_ref[A_HEAD_DIM:A_HEAD_DIM + 1, :]
    o = jnp.concatenate([o[:, lanes[h]] for h in heads], axis=0)
    o_ref[0] = o.T.astype(BF16)


def _dsa(qt, qit, wt, k, ki, vt):
    bsz, n_blk, _, hq = qt.shape
    seq = n_blk * Q_BLOCK
    n_sel = min(TOPK_MAX, seq // 4)
    n_kc = seq // KEY_CHUNK
    hd = A_HEADS * A_HEAD_DIM
    kern = functools.partial(_dsa_kernel, n_sel=n_sel, search_steps=32)
    return pl.pallas_call(
        kern,
        grid=(bsz, n_blk),
        in_specs=[pl.BlockSpec((1, 1, A_HEAD_DIM, hq), lambda b, i: (b, i, 0, 0)),
                  pl.BlockSpec((1, 1, IDX_DIM, hq), lambda b, i: (b, i, 0, 0)),
                  pl.BlockSpec((1, IDX_HEADS, Q_BLOCK), lambda b, i: (b, 0, i)),
                  pl.BlockSpec((1, seq, A_HEAD_DIM), lambda b, i: (b, 0, 0)),
                  pl.BlockSpec((1, seq, IDX_DIM), lambda b, i: (b, 0, 0)),
                  pl.BlockSpec((1, n_kc, A_HEAD_DIM, KEY_CHUNK), lambda b, i: (b, 0, 0, 0))],
        out_specs=pl.BlockSpec((1, Q_BLOCK, hd), lambda b, i: (b, i, 0)),
        out_shape=jax.ShapeDtypeStruct((bsz, seq, hd), BF16),
        scratch_shapes=[pltpu.VMEM((n_kc, KEY_CHUNK, Q_BLOCK), F32),
                        pltpu.VMEM((n_kc, KEY_CHUNK, hq), F32),
                        pltpu.VMEM((A_HEAD_DIM + BF16_ROWS, hq), F32)],
        compiler_params=_params("arbitrary", "arbitrary"),
        name="dsa",
    )(qt, qit, wt, k, ki, vt)


def _softplus(x):
    return jnp.maximum(x, 0.0) + jnp.log(1.0 + jnp.exp(-jnp.abs(x)))


def _split3(x):
    hi = x.astype(BF16)
    r = x - hi.astype(F32)
    mid = r.astype(BF16)
    lo = (r - mid.astype(F32)).astype(BF16)
    return hi, mid, lo


def _gdn_kernel(q_ref, k_ref, v_ref, z_ref, ab_ref, alog_ref, dtb_ref, ng_ref,
                o_ref, st_ref, *, n_steps, heads):
    c = GDN_CHUNK
    dk = B_KEY_DIM

    st_ref[...] = jnp.zeros(st_ref.shape, F32)
    ri = lax.broadcasted_iota(jnp.int32, (c, c), 0)
    ci = lax.broadcasted_iota(jnp.int32, (c, c), 1)
    strict = ri > ci
    eye = jnp.where(ri == ci, 1.0, 0.0)
    incl_ones = jnp.where(ri >= ci, 1.0, 0.0).astype(BF16)
    inv_shift = GDN_INV_BLOCK.bit_length() - 1
    diag_mask = jnp.where((ri >> inv_shift) == (ci >> inv_shift), 1.0, 0.0)
    level_masks = []
    for ls in range(inv_shift, c.bit_length() - 1):
        lower_left = jnp.where((ri >> ls) == (ci >> ls) + 1, 1.0, 0.0) * jnp.where(((ci >> ls) & 1) == 0, 1.0, 0.0)
        level_masks.append(lower_left.astype(BF16))

    def step(s, carry):
        rows = pl.ds(pl.multiple_of(s * c, c), c)
        ab = ab_ref[0, rows, :]
        log_a = -jnp.exp(alog_ref[...]) * _softplus(ab + dtb_ref[...])
        beta_all = 1.0 / (1.0 + jnp.exp(-ab))
        la_hi, la_mid, la_lo = _split3(log_a)
        gam_all = _dot(incl_ones, la_hi) + _dot(incl_ones, la_mid) + _dot(incl_ones, la_lo)

        hs = range(heads)
        gam = [gam_all[:, h:h + 1] for h in hs]
        beta = [beta_all[:, B_HEADS + h:B_HEADS + h + 1] for h in hs]
        cols = [slice(h * dk, (h + 1) * dk) for h in hs]
        kh = [k_ref[0, rows, cols[h]] for h in hs]
        qh = [q_ref[0, rows, cols[h]] * (dk ** -0.5) for h in hs]
        vh = [v_ref[0, rows, cols[h]] for h in hs]
        kb = [kh[h].astype(BF16) for h in hs]
        kk = [_dot_nt(kb[h], kb[h]) for h in hs]
        qk = [_dot_nt(qh[h].astype(BF16), kb[h]) for h in hs]

        dec = []
        for h in hs:
            gcol = jnp.broadcast_to(gam[h], (c, c))
            diff = gcol - gcol.T
            dec.append(jnp.where(strict, jnp.exp(jnp.where(strict, diff, 0.0)), 0.0))
        nmat = [beta[h] * kk[h] * dec[h] for h in hs]
        a_qk = [(qk[h] * (dec[h] + eye)).astype(BF16) for h in hs]

        mp = [-(nmat[h] * diag_mask) for h in hs]
        x = [eye + mp[h] for h in hs]
        for _ in range(GDN_INV_BLOCK.bit_length() - 2):
            mpb = [mp[h].astype(BF16) for h in hs]
            mp = [_dot(mpb[h], mpb[h]) for h in hs]
            x = [x[h] + _dot(x[h].astype(BF16), mp[h].astype(BF16)) for h in hs]
        nb = [nmat[h].astype(BF16) for h in hs]
        for lm in level_masks:
            xb = [x[h].astype(BF16) for h in hs]
            cx = [_dot(nb[h] * lm, xb[h]).astype(BF16) for h in hs]
            x = [x[h] - _dot(xb[h], cx[h]) for h in hs]
        xb = [x[h].astype(BF16) for h in hs]

        g_cum = [jnp.exp(gam[h]) for h in hs]
        w = [_dot(xb[h], ((beta[h] * g_cum[h]) * kh[h]).astype(BF16)).astype(BF16) for h in hs]
        u = [_dot(xb[h], (beta[h] * vh[h]).astype(BF16)) for h in hs]
        g_last = [gam[h][c - 1:c, :] for h in hs]
        q_dec = [(qh[h] * g_cum[h]).astype(BF16) for h in hs]
        k_dec = [(kh[h] * jnp.exp(g_last[h] - gam[h])).astype(BF16) for h in hs]

        state = [st_ref[h] for h in hs]
        sb = [state[h].astype(BF16) for h in hs]
        db = [(u[h] - _dot(w[h], sb[h])).astype(BF16) for h in hs]
        out = [_dot(q_dec[h], sb[h]) + _dot(a_qk[h], db[h]) for h in hs]
        for h in hs:
            st_ref[h] = jnp.exp(g_last[h]) * state[h] + _dot_tn(k_dec[h], db[h])
        for h in hs:
            normed = (out[h] * lax.rsqrt(jnp.mean(out[h] * out[h], axis=-1, keepdims=True) + RMS_EPS)
                      * ng_ref[...])
            o_ref[0, rows, cols[h]] = (normed * _silu(z_ref[0, rows, cols[h]])).astype(BF16)
        return carry

    lax.fori_loop(0, n_steps, step, 0)


def _gdn(qkv, z, ab, a_log, dt_bias, norm_g):
    bsz, seq, _ = qkv.shape
    heads = B_HEADS
    gw = heads * B_KEY_DIM
    pad_lanes = lambda v: jnp.pad(v.reshape(1, -1), ((0, 0), (0, LANES - v.size)))
    kern = functools.partial(_gdn_kernel, n_steps=seq // GDN_CHUNK, heads=heads)
    col_block = lambda off: (lambda b: (b, 0, off))
    const2 = lambda b: (0, 0)
    return pl.pallas_call(
        kern,
        grid=(bsz,),
        in_specs=[pl.BlockSpec((1, seq, gw), col_block(0)),
                  pl.BlockSpec((1, seq, gw), col_block(1)),
                  pl.BlockSpec((1, seq, gw), col_block(2)),
                  pl.BlockSpec((1, seq, gw), col_block(0)),
                  pl.BlockSpec((1, seq, LANES), lambda b: (b, 0, 0)),
                  pl.BlockSpec((1, LANES), const2),
                  pl.BlockSpec((1, LANES), const2),
                  pl.BlockSpec((1, B_VAL_DIM), const2)],
        out_specs=pl.BlockSpec((1, seq, gw), col_block(0)),
        out_shape=jax.ShapeDtypeStruct((bsz, seq, B_HEADS * B_VAL_DIM), BF16),
        scratch_shapes=[pltpu.VMEM((heads, B_KEY_DIM, B_VAL_DIM), F32)],
        compiler_params=_params("arbitrary"),
        name="gdn",
    )(qkv, qkv, qkv, z, ab, pad_lanes(a_log), pad_lanes(dt_bias), norm_g.reshape(1, B_VAL_DIM))


def _outproj_kernel(x_ref, mod_ref, attn_ref, dn_ref, wo_ref, g_ref, b_ref, o_ref, *, n_out_chunks):
    gate = mod_ref[0, 5:6, :]
    rows_per = x_ref.shape[1] // n_out_chunks
    for r in range(n_out_chunks):
        rows = slice(r * rows_per, (r + 1) * rows_per)
        mixed = jnp.concatenate([attn_ref[0, rows, :], dn_ref[0, rows, :]], axis=1)
        y = DEEPNORM_ALPHA * x_ref[0, rows, :] + gate * _dot(mixed, wo_ref[...])
        o_ref[0, rows, :] = _layer_norm(y, g_ref[...], b_ref[...])


def _outproj(x, mod, attn, dn, w_out, ln_g, ln_b):
    bsz, seq, d = x.shape
    tm = min(OUT_TOKENS, seq)
    tok = lambda b, i: (b, i, 0)
    const2 = lambda b, i: (0, 0)
    kern = functools.partial(_outproj_kernel, n_out_chunks=max(1, tm // FFN_OUT_ROWS))
    return pl.pallas_call(
        kern,
        grid=(bsz, seq // tm),
        in_specs=[pl.BlockSpec((1, tm, d), tok),
                  pl.BlockSpec((1, N_MOD, d), lambda b, i: (b, 0, 0)),
                  pl.BlockSpec((1, tm, attn.shape[2]), tok),
                  pl.BlockSpec((1, tm, dn.shape[2]), tok),
                  pl.BlockSpec(w_out.shape, const2),
                  pl.BlockSpec((1, d), const2),
                  pl.BlockSpec((1, d), const2)],
        out_specs=pl.BlockSpec((1, tm, d), tok),
        out_shape=jax.ShapeDtypeStruct((bsz, seq, d), F32),
        compiler_params=_params("arbitrary", "arbitrary"),
        name="outproj",
    )(x, mod, attn, dn, w_out.astype(BF16), ln_g.reshape(1, d), ln_b.reshape(1, d))


def kernel(x, c, positions, w_ada, b_ada, ffn1_w1, ffn1_w3, ffn1_w2, ln1_g, ln1_b, w_in, conv_w, a_log, dt_bias, dn_norm_g, w_out, ln2_g, ln2_b, ffn2_w1, ffn2_w3, ffn2_w2, ln3_g, ln3_b):
    bsz, seq, d = x.shape
    pos_f = positions.astype(F32).reshape(bsz, 1, seq)
    for layer in range(w_ada.shape[0]):
        mod = _adaln(c, w_ada[layer], b_ada[layer]).reshape(bsz, N_MOD, d)
        x = _ffn_block(x, mod, ffn1_w1[layer], ffn1_w3[layer], ffn1_w2[layer], ln1_g[layer], ln1_b[layer],
                       mod_base=0)
        qt, qit, wt, k, ki, vt, qkv, z, ab = _proj(x, mod, pos_f, w_in[layer], conv_w[layer])
        attn = _dsa(qt, qit, wt, k, ki, vt)
        dn = _gdn(qkv, z, ab, a_log[layer], dt_bias[layer], dn_norm_g[layer])
        x = _outproj(x, mod, attn, dn, w_out[layer], ln2_g[layer], ln2_b[layer])
        x = _ffn_block(x, mod, ffn2_w1[layer], ffn2_w3[layer], ffn2_w2[layer], ln3_g[layer], ln3_b[layer],
                       mod_base=6)
    return x
```

```python
import functools

import jax
import jax.numpy as jnp
from jax import lax
from jax.experimental import pallas as pl
from jax.experimental.pallas import tpu as pltpu

F32 = jnp.float32
BF16 = jnp.bfloat16

A_HEADS = 8
A_HEAD_DIM = 64
IDX_HEADS = 8
IDX_DIM = 64
TOPK_MAX = 256
ROPE_THETA = 10000.0
B_HEADS = 4
B_KEY_DIM = 128
B_VAL_DIM = 128
CONV_WIDTH = 4
MACARON_WEIGHT = 0.5
DEPTH = 1
DEEPNORM_ALPHA = (2.0 * DEPTH) ** 0.25
LN_EPS = 1e-5
RMS_EPS = 1e-6
N_MOD = 9

LANES = 128
SUBLANES = 8
BF16_ROWS = 16
VMEM_LIMIT_BYTES = 56 * 1024 * 1024

FFN_TOKENS = 1024
FFN_CHUNK = 256
FFN_OUT_ROWS = 512
PROJ_TOKENS = 512
PROJ_CONV_COLS = 256
PROJ_CONV_ROWS = 128
OUT_TOKENS = 1024
ROW_CHUNK = 256
Q_BLOCK = 256
KEY_CHUNK = 256
GDN_CHUNK = 256
GDN_INV_BLOCK = 4

COUNT_ROWS = 32

NEG_INF = float("-inf")


def _silu(x):
    return x * (1.0 / (1.0 + jnp.exp(-x)))


def _dot(a, b):
    return jnp.dot(a, b, preferred_element_type=F32)


def _dot_nt(a, b):
    return lax.dot_general(a, b, (((1,), (1,)), ((), ())), preferred_element_type=F32)


def _dot_tn(a, b):
    return lax.dot_general(a, b, (((0,), (0,)), ((), ())), preferred_element_type=F32)


def _layer_norm(y, g, b):
    mu = jnp.mean(y, axis=-1, keepdims=True)
    d = y - mu
    var = jnp.mean(d * d, axis=-1, keepdims=True)
    return d * lax.rsqrt(var + LN_EPS) * g + b


def _params(*sem):
    return pltpu.CompilerParams(dimension_semantics=sem, vmem_limit_bytes=VMEM_LIMIT_BYTES)


def _adaln_kernel(c_ref, w_ref, b_ref, o_ref):
    sc = _silu(c_ref[...]).astype(BF16)
    o_ref[...] = _dot(sc, w_ref[...].astype(BF16)) + b_ref[...]


def _adaln(c, w, b):
    bsz, d = c.shape
    n = w.shape[1]
    tn = n // 8
    return pl.pallas_call(
        _adaln_kernel,
        grid=(n // tn,),
        in_specs=[pl.BlockSpec((bsz, d), lambda j: (0, 0)),
                  pl.BlockSpec((d, tn), lambda j: (0, j)),
                  pl.BlockSpec((1, tn), lambda j: (0, j))],
        out_specs=pl.BlockSpec((bsz, tn), lambda j: (0, j)),
        out_shape=jax.ShapeDtypeStruct((bsz, n), F32),
        compiler_params=_params("arbitrary"),
        name="adaln",
    )(c, w, b.reshape(1, n))


def _ffn_kernel(x_ref, mod_ref, w1_ref, w3_ref, w2_ref, g_ref, b_ref, o_ref, u_ref, hid_ref,
                *, mod_base, n_chunks, n_row_chunks, n_out_chunks):
    shift = mod_ref[0, mod_base:mod_base + 1, :]
    scale = mod_ref[0, mod_base + 1:mod_base + 2, :]
    gate = mod_ref[0, mod_base + 2:mod_base + 3, :]

    def prologue(r, carry):
        rows = pl.ds(pl.multiple_of(r * ROW_CHUNK, ROW_CHUNK), ROW_CHUNK)
        u_ref[rows, :] = (x_ref[0, rows, :] * (1.0 + scale) + shift).astype(BF16)
        return carry

    lax.fori_loop(0, n_row_chunks, prologue, 0)

    u = u_ref[...]
    for f in range(n_chunks):
        cols = slice(f * FFN_CHUNK, (f + 1) * FFN_CHUNK)
        a = _dot(u, w1_ref[:, cols])
        b = _dot(u, w3_ref[:, cols])
        hid_ref[:, cols] = (_silu(a) * b).astype(BF16)

    rows_per = x_ref.shape[1] // n_out_chunks
    for r in range(n_out_chunks):
        rows = slice(r * rows_per, (r + 1) * rows_per)
        h = _dot(hid_ref[rows, :], w2_ref[...])
        y =DEEPNORM_ALPHA * x_ref[0, rows, :] + (MACARON_WEIGHT * gate) * h
        o_ref[0, rows, :] = _layer_norm(y, g_ref[...], b_ref[...])


def _ffn_block(x, mod, w1, w3, w2, ln_g, ln_b, *, mod_base):
    bsz, seq, d = x.shape
    dff = w1.shape[1]
    n_chunks = dff // FFN_CHUNK
    tm = min(FFN_TOKENS, seq)
    kern = functools.partial(_ffn_kernel, mod_base=mod_base, n_chunks=n_chunks,
                             n_row_chunks=tm // ROW_CHUNK, n_out_chunks=max(1, tm // FFN_OUT_ROWS))
    const2 = lambda b, i: (0, 0)
    return pl.pallas_call(
        kern,
        grid=(bsz, seq // tm),
        in_specs=[pl.BlockSpec((1, tm, d), lambda b, i: (b, i, 0)),
                  pl.BlockSpec((1, N_MOD, d), lambda b, i: (b, 0, 0)),
                  pl.BlockSpec((d, dff), const2, pipeline_mode=pl.Buffered(1)),
                  pl.BlockSpec((d, dff), const2, pipeline_mode=pl.Buffered(1)),
                  pl.BlockSpec((dff, d), const2, pipeline_mode=pl.Buffered(1)),
                  pl.BlockSpec((1, d), lambda b, i: (0, 0)),
                  pl.BlockSpec((1, d), lambda b, i: (0, 0))],
        out_specs=pl.BlockSpec((1, tm, d), lambda b, i: (b, i, 0)),
        out_shape=jax.ShapeDtypeStruct((bsz, seq, d), F32),
        scratch_shapes=[pltpu.VMEM((tm, d), BF16), pltpu.VMEM((tm, dff), BF16)],
        compiler_params=_params("arbitrary", "arbitrary"),
        name="ffn",
    )(x, mod, w1.astype(BF16), w3.astype(BF16), w2.astype(BF16), ln_g.reshape(1, d), ln_b.reshape(1, d))


_ROW_Q = 0
_ROW_QI = _ROW_Q + A_HEADS * A_HEAD_DIM
_ROW_K = _ROW_QI + IDX_HEADS * IDX_DIM
_ROW_KI = _ROW_K + A_HEAD_DIM
_ROW_V = _ROW_KI + IDX_DIM
_ROW_W = _ROW_V + A_HEAD_DIM
_ROWS_A = _ROW_W + IDX_HEADS
_ROWS_A_PADDED = -(-_ROWS_A // 16) * 16
_COL_QKV = 0
_COL_Z = 2 * B_HEADS * B_KEY_DIM + B_HEADS * B_VAL_DIM
_COL_AB = _COL_Z + B_HEADS * B_VAL_DIM
_COLS_B = _COL_AB + LANES


def _proj_kernel(x_ref, mod_ref, pos_ref, invf_ref, wat_ref, wb_ref, cw_ref,
                 qt_ref, qit_ref, wt_ref, k_ref, ki_ref, vt_ref, qkv_ref, z_ref, ab_ref,
                 u_ref, pat_ref, prev_ref, *, n_row_chunks):
    shift = mod_ref[0, 3:4, :]
    scale = mod_ref[0, 4:5, :]
    tm = x_ref.shape[1]

    def prologue(r, carry):
        rows = pl.ds(pl.multiple_of(r * ROW_CHUNK, ROW_CHUNK), ROW_CHUNK)
        u_ref[rows, :] = (x_ref[0, rows, :] * (1.0 + scale) + shift).astype(BF16)
        return carry

    lax.fori_loop(0, n_row_chunks, prologue, 0)
    u = u_ref[...]

    halo = SUBLANES

    @pl.when(pl.program_id(1) == 0)
    def _():
        prev_ref[...] = jnp.zeros(prev_ref.shape, F32)

    n_normed = 2 * B_HEADS * B_KEY_DIM
    rb = PROJ_CONV_ROWS
    for c0 in range(_COL_QKV, _COL_Z, PROJ_CONV_COLS):
        cols = slice(c0, c0 + PROJ_CONV_COLS)
        prev = prev_ref[:, cols]
        for r0 in range(0, tm, rb):
            raw = _dot(u_ref[r0:r0 + rb, :], wb_ref[:, cols])
            xw = jnp.concatenate([prev, raw], axis=0)
            prev = raw[rb - halo:, :]
            y = raw * cw_ref[CONV_WIDTH - 1:CONV_WIDTH, cols]
            for j in range(CONV_WIDTH - 1):
                sh = CONV_WIDTH - 1 - j
                y = y + xw[halo - sh:halo - sh + rb] * cw_ref[j:j + 1, cols]
            y = _silu(y)
            for h0 in range(0, PROJ_CONV_COLS, B_KEY_DIM):
                t = y[:, h0:h0 + B_KEY_DIM]
                if c0 + h0 < n_normed:
                    t = t * lax.rsqrt(jnp.sum(t * t, axis=-1, keepdims=True) + RMS_EPS)
                qkv_ref[0, r0:r0 + rb, c0 + h0:c0 + h0 + B_KEY_DIM] = t
        prev_ref[:, cols] = prev
    z_ref[0] = _dot(u, wb_ref[:, _COL_Z:_COL_AB])
    ab_ref[0] = _dot(u, wb_ref[:, _COL_AB:_COLS_B])

    ang = invf_ref[...] * pos_ref[0]
    cos = jnp.cos(ang)
    sin = jnp.sin(ang)
    half = A_HEAD_DIM // 2

    pat_ref[...] = _dot_nt(wat_ref[...], u)

    def roped(row0):
        t1 = pat_ref[row0:row0 + half, :]
        t2 = pat_ref[row0 + half:row0 + A_HEAD_DIM, :]
        return t1 * cos - t2 * sin, t2 * cos + t1 * sin

    def store_heads(dst_ref, row_base, scale):
        for h in range(A_HEADS):
            o1, o2 = roped(row_base + h * A_HEAD_DIM)
            o1 = (o1 * scale).astype(BF16)
            o2 = (o2 * scale).astype(BF16)
            lanes = slice(h * Q_BLOCK, (h + 1) * Q_BLOCK)
            for j in range(dst_ref.shape[1]):
                toks = slice(j * Q_BLOCK, (j + 1) * Q_BLOCK)
                dst_ref[0, j, :half, lanes] = o1[:, toks]
                dst_ref[0, j, half:, lanes] = o2[:, toks]

    store_heads(qt_ref, _ROW_Q, A_HEAD_DIM ** -0.5)
    store_heads(qit_ref, _ROW_QI, 1.0)

    k1, k2 = roped(_ROW_K)
    ki1, ki2 = roped(_ROW_KI)
    kk = jnp.concatenate([k1, k2, ki1, ki2], axis=0).T
    k_ref[0] = kk[:, :A_HEAD_DIM].astype(BF16)
    ki_ref[0] = kk[:, A_HEAD_DIM:].astype(BF16)

    for j in range(vt_ref.shape[1]):
        vt_ref[0, j] = pat_ref[_ROW_V:_ROW_V + A_HEAD_DIM, j * KEY_CHUNK:(j + 1) * KEY_CHUNK].astype(BF16)

    wt_ref[0] = pat_ref[_ROW_W:_ROW_W + IDX_HEADS, :] * (IDX_HEADS ** -0.5 * IDX_DIM ** -0.5)


def _proj(x, mod, pos_f, w_in, conv_w):
    bsz, seq, d = x.shape
    tm = min(PROJ_TOKENS, seq)
    splits = (A_HEADS * A_HEAD_DIM, A_HEAD_DIM, A_HEAD_DIM, IDX_HEADS * IDX_DIM, IDX_DIM, IDX_HEADS,
              B_HEADS * B_KEY_DIM, B_HEADS * B_KEY_DIM, B_HEADS * B_VAL_DIM, B_HEADS * B_VAL_DIM,
              B_HEADS, B_HEADS)
    offs = [0]
    for s in splits:
        offs.append(offs[-1] + s)
    cols = [w_in[:, offs[i]:offs[i + 1]] for i in range(len(splits))]
    a_q, a_k, a_v, i_q, i_k, i_w, b_q, b_k, b_v, b_z, b_a, b_b = cols
    wat = jnp.concatenate([a_q, i_q, a_k, i_k, a_v, i_w], axis=1).T.astype(BF16)
    wat = jnp.pad(wat, ((0, _ROWS_A_PADDED - _ROWS_A), (0, 0)))
    wb = jnp.concatenate([b_q, b_k, b_v, b_z, b_a, b_b], axis=1).astype(BF16)
    wb = jnp.pad(wb, ((0, 0), (0, _COLS_B - wb.shape[1])))
    half = A_HEAD_DIM // 2
    inv_freq = (ROPE_THETA ** (-jnp.arange(0, A_HEAD_DIM, 2, dtype=F32) / A_HEAD_DIM)).reshape(half, 1)

    n_kc = tm // KEY_CHUNK
    qkv_w = _COL_Z
    z_w = _COL_AB - _COL_Z
    tok = lambda b, i: (b, i, 0)
    tok_t = lambda b, i: (b, 0, i)
    const2 = lambda b, i: (0, 0)
    n_qb = tm // Q_BLOCK
    q_spec = pl.BlockSpec((1, n_qb, A_HEAD_DIM, A_HEADS * Q_BLOCK), lambda b, i: (b, i, 0, 0))
    out_shapes = (
        jax.ShapeDtypeStruct((bsz, seq // Q_BLOCK, A_HEAD_DIM, A_HEADS * Q_BLOCK), BF16),
        jax.ShapeDtypeStruct((bsz, seq // Q_BLOCK, IDX_DIM, IDX_HEADS * Q_BLOCK), BF16),
        jax.ShapeDtypeStruct((bsz, IDX_HEADS, seq), F32),
        jax.ShapeDtypeStruct((bsz, seq, A_HEAD_DIM), BF16),
        jax.ShapeDtypeStruct((bsz, seq, IDX_DIM), BF16),
        jax.ShapeDtypeStruct((bsz, seq // KEY_CHUNK, A_HEAD_DIM, KEY_CHUNK), BF16),
        jax.ShapeDtypeStruct((bsz, seq, qkv_w), F32),
        jax.ShapeDtypeStruct((bsz, seq, z_w), F32),
        jax.ShapeDtypeStruct((bsz, seq, LANES), F32),
    )
    out_specs = (
        q_spec,
        q_spec,
        pl.BlockSpec((1, IDX_HEADS, tm), tok_t),
        pl.BlockSpec((1, tm, A_HEAD_DIM), tok),
        pl.BlockSpec((1, tm, IDX_DIM), tok),
        pl.BlockSpec((1, n_kc, A_HEAD_DIM, KEY_CHUNK), lambda b, i: (b, i, 0, 0)),
        pl.BlockSpec((1, tm, qkv_w), tok),
        pl.BlockSpec((1, tm, z_w), tok),
        pl.BlockSpec((1, tm, LANES), tok),
    )
    kern = functools.partial(_proj_kernel, n_row_chunks=tm // ROW_CHUNK)
    return pl.pallas_call(
        kern,
        grid=(bsz, seq // tm),
        in_specs=[pl.BlockSpec((1, tm, d), tok),
                  pl.BlockSpec((1, N_MOD, d), lambda b, i: (b, 0, 0)),
                  pl.BlockSpec((1, 1, tm), tok_t),
                  pl.BlockSpec((half, 1), const2),
                  pl.BlockSpec((_ROWS_A_PADDED, d), const2),
                  pl.BlockSpec((d, _COLS_B), const2),
                  pl.BlockSpec(conv_w.shape, const2)],
        out_specs=out_specs,
        out_shape=out_shapes,
        scratch_shapes=[pltpu.VMEM((tm, d), BF16), pltpu.VMEM((_ROWS_A_PADDED, tm), F32),
                        pltpu.VMEM((SUBLANES, qkv_w), F32)],
        compiler_params=_params("arbitrary", "arbitrary"),
        name="proj",
    )(x, mod, pos_f, inv_freq, wat, wb, conv_w)


_KEY_LO = -2139095041
_KEY_HI = 2139095041


def _key_to_float(key):
    bits = jnp.where(key < 0, key ^ 0x7FFFFFFF, key)
    return lax.bitcast_convert_type(bits, F32)


def _fold_rows(t):
    return t.reshape(t.shape[0] // SUBLANES, SUBLANES, t.shape[1])


def _dsa_kernel(qt_ref, qit_ref, wt_ref, k_ref, ki_ref, vt_ref, o_ref, s_ref, lg_ref, ot_ref,
                *, n_sel, search_steps):
    blk = pl.program_id(1)
    kc = KEY_CHUNK
    q_base = blk * Q_BLOCK
    n_keys = q_base + Q_BLOCK
    n_kc = (n_keys + kc - 1) // kc
    row = lax.broadcasted_iota(jnp.int32, (kc, Q_BLOCK), 0)
    col = lax.broadcasted_iota(jnp.int32, (kc, Q_BLOCK), 1)
    wt = wt_ref[0]
    heads = range(A_HEADS)
    lanes = [slice(h * Q_BLOCK, (h + 1) * Q_BLOCK) for h in heads]

    def chunk_rows(c):
        if isinstance(c, int):
            return slice(c * kc, (c + 1) * kc)
        return pl.ds(pl.multiple_of(c * kc, kc), kc)

    def per_chunk_count(make_variant):
        return lax.switch(n_kc - 1, [make_variant(n) for n in range(1, s_ref.shape[0] + 1)])

    def score_chunk(c):
        x = _dot(ki_ref[0, chunk_rows(c), :], qit_ref[0, 0])
        acc = wt[0:1, :] * jnp.maximum(x[:, lanes[0]], 0.0)
        for h in range(1, IDX_HEADS):
            acc = acc + wt[h:h + 1, :] * jnp.maximum(x[:, lanes[h]], 0.0)
        causal = row + (c * kc - q_base) <= col
        s_ref[c] = jnp.where(causal, acc, NEG_INF)

    def score_all(n):
        def run():
            for c in range(n):
                score_chunk(c)
            return 0
        return run

    per_chunk_count(score_all)

    def count(pred):
        def body(c, acc):
            ind = jnp.where(pred(s_ref[c]), 1.0, 0.0)
            return acc + jnp.sum(ind.reshape(kc // COUNT_ROWS, COUNT_ROWS, Q_BLOCK), axis=0)

        acc = lax.fori_loop(0, n_kc, body, jnp.zeros((COUNT_ROWS, Q_BLOCK), F32))
        return jnp.sum(acc, axis=0, keepdims=True)

    def rewrite_scores(fn):
        def body(c, carry):
            s_ref[c] = fn(s_ref[c])
            return carry
        lax.fori_loop(0, n_kc, body, 0)

    @pl.when(n_keys <= n_sel)
    def _():
        rewrite_scores(lambda s: jnp.where(s > NEG_INF, 0.0, NEG_INF))

    @pl.when(n_keys > n_sel)
    def _():
        lo0 = jnp.full((1, Q_BLOCK), _KEY_LO, jnp.int32)
        hi0 = jnp.full((1, Q_BLOCK), _KEY_HI, jnp.int32)
        cnt0 = jnp.full((1, Q_BLOCK), float(n_sel), F32)

        def search_over(n):
            def count_ge(thr):
                acc = jnp.zeros((COUNT_ROWS, Q_BLOCK), F32)
                for c in range(n):
                    ind = jnp.where(s_ref[c] >= thr, 1.0, 0.0)
                    acc = acc + jnp.sum(ind.reshape(kc // COUNT_ROWS, COUNT_ROWS, Q_BLOCK), axis=0)
                return jnp.sum(acc, axis=0, keepdims=True)

            def step(i, carry):
                lo, hi, cnt_lo = carry
                mid = (lo & hi) + ((lo ^ hi) >> 1)
                cnt = count_ge(_key_to_float(mid))
                ok = cnt >= n_sel
                return jnp.where(ok, mid, lo), jnp.where(ok, hi, mid), jnp.where(ok, cnt, cnt_lo)

            return lambda: lax.fori_loop(0, search_steps, step, (lo0, hi0, cnt0))

        lo, _, cnt_lo = lax.switch(n_kc - 1, [search_over(n) for n in range(1, s_ref.shape[0] + 1)])
        thr = _key_to_float(lo)
        has_ties = jnp.max(cnt_lo) > n_sel

        @pl.when(jnp.logical_not(has_ties))
        def _():
            rewrite_scores(lambda s: jnp.where(s >= thr, jnp.where(s > NEG_INF, 0.0, NEG_INF), NEG_INF))

        @pl.when(has_ties)
        def _():
            quota = n_sel - count(lambda s: s > thr)
            lower = (lax.broadcasted_iota(jnp.int32, (kc, kc), 1)
                     < lax.broadcasted_iota(jnp.int32, (kc, kc), 0))
            lower = jnp.where(lower, 1.0, 0.0).astype(BF16)

            def body(c, seen):
                s = s_ref[c]
                eq = jnp.where(s == thr, 1.0, 0.0)
                rank = _dot(lower, eq.astype(BF16)) + seen
                take = jnp.where(s > thr, 1.0, jnp.where(rank < quota, eq, 0.0))
                s_ref[c] = jnp.where(take > 0.5, jnp.where(s > NEG_INF, 0.0, NEG_INF), NEG_INF)
                return seen + jnp.sum(eq, axis=0, keepdims=True)

            lax.fori_loop(0, n_kc, body, jnp.zeros((1, Q_BLOCK), F32))

    def logits_chunk(c, m):
        bias = s_ref[c]
        lg = _dot(k_ref[0, chunk_rows(c), :], qt_ref[0, 0]) + jnp.concatenate([bias] * A_HEADS, axis=1)
        lg_ref[c] = lg
        return jnp.maximum(m, jnp.max(_fold_rows(lg), axis=0))

    def logits_all(n):
        def run():
            m = jnp.full((SUBLANES, A_HEADS * Q_BLOCK), NEG_INF, F32)
            for c in range(n):
                m = logits_chunk(c, m)
            return m
        return run

    m = jnp.max(per_chunk_count(logits_all), axis=0, keepdims=True)

    n_ones = ot_ref.shape[0] - A_HEAD_DIM

    def pv_all(n):
        def run():
            acc = None
            for c0 in range(0, n, 2):
                cs = range(c0, min(c0 + 2, n))
                lg = jnp.concatenate([lg_ref[c] for c in cs], axis=0)
                p = jnp.exp((lg - m).astype(BF16))
                vt_aug = jnp.concatenate([jnp.concatenate([vt_ref[0, c] for c in cs], axis=1),
                                          jnp.ones((n_ones, len(cs) * kc), BF16)], axis=0)
                part = _dot(vt_aug, p)
                acc = part if acc is None else acc + part
            ot_ref[...] = acc
            return 0
        return run

    per_chunk_count(pv_all)
    o = ot_ref[:A_HEAD_DIM, :] / ot_ref[A_HEAD_DIM:A_HEAD_DIM + 1, :]
    o = jnp.concatenate([o[:, lanes[h]] for h in heads], axis=0)
    o_ref[0] = o.T.astype(BF16)


def _dsa(qt, qit, wt, k, ki, vt):
    bsz, n_blk, _, hq = qt.shape
    seq = n_blk * Q_BLOCK
    n_sel = min(TOPK_MAX, seq // 4)
    n_kc = seq // KEY_CHUNK
    hd = A_HEADS * A_HEAD_DIM
    kern = functools.partial(_dsa_kernel, n_sel=n_sel, search_steps=32)
    return pl.pallas_call(
        kern,
        grid=(bsz, n_blk),
        in_specs=[pl.BlockSpec((1, 1, A_HEAD_DIM, hq), lambda b, i: (b, i, 0, 0)),
                  pl.BlockSpec((1, 1, IDX_DIM, hq), lambda b, i: (b, i, 0, 0)),
                  pl.BlockSpec((1, IDX_HEADS, Q_BLOCK), lambda b, i: (b, 0, i)),
                  pl.BlockSpec((1, seq, A_HEAD_DIM), lambda b, i: (b, 0, 0)),
                  pl.BlockSpec((1, seq, IDX_DIM), lambda b, i: (b, 0, 0)),
                  pl.BlockSpec((1, n_kc, A_HEAD_DIM, KEY_CHUNK), lambda b, i: (b, 0, 0, 0))],
        out_specs=pl.BlockSpec((1, Q_BLOCK, hd), lambda b, i: (b, i, 0)),
        out_shape=jax.ShapeDtypeStruct((bsz, seq, hd), BF16),
        scratch_shapes=[pltpu.VMEM((n_kc, KEY_CHUNK, Q_BLOCK), F32),
                        pltpu.VMEM((n_kc, KEY_CHUNK, hq), F32),
                        pltpu.VMEM((A_HEAD_DIM + BF16_ROWS, hq), F32)],
        compiler_params=_params("arbitrary", "arbitrary"),
        name="dsa",
    )(qt, qit, wt, k, ki, vt)


def _softplus(x):
    return jnp.maximum(x, 0.0) + jnp.log(1.0 + jnp.exp(-jnp.abs(x)))


def _split3(x):
    hi = x.astype(BF16)
    r = x - hi.astype(F32)
    mid = r.astype(BF16)
    lo = (r - mid.astype(F32)).astype(BF16)
    return hi, mid, lo


def _gdn_kernel(q_ref, k_ref, v_ref, z_ref, ab_ref, alog_ref, dtb_ref, ng_ref,
                o_ref, st_ref, *, n_steps, heads):
    c = GDN_CHUNK
    dk = B_KEY_DIM

    st_ref[...] = jnp.zeros(st_ref.shape, F32)
    ri = lax.broadcasted_iota(jnp.int32, (c, c), 0)
    ci = lax.broadcasted_iota(jnp.int32, (c, c), 1)
    strict = ri > ci
    eye = jnp.where(ri == ci, 1.0, 0.0)
    incl_ones = jnp.where(ri >= ci, 1.0, 0.0).astype(BF16)
    inv_shift = GDN_INV_BLOCK.bit_length() - 1
    diag_mask = jnp.where((ri >> inv_shift) == (ci >> inv_shift), 1.0, 0.0)
    level_masks = []
    for ls in range(inv_shift, c.bit_length() - 1):
        lower_left = jnp.where((ri >> ls) == (ci >> ls) + 1, 1.0, 0.0) * jnp.where(((ci >> ls) & 1) == 0, 1.0, 0.0)
        level_masks.append(lower_left.astype(BF16))

    def step(s, carry):
        rows = pl.ds(pl.multiple_of(s * c, c), c)
        ab = ab_ref[0, rows, :]
        log_a = -jnp.exp(alog_ref[...]) * _softplus(ab + dtb_ref[...])
        beta_all = 1.0 / (1.0 + jnp.exp(-ab))
        la_hi, la_mid, la_lo = _split3(log_a)
        gam_all = _dot(incl_ones, la_hi) + _dot(incl_ones, la_mid) + _dot(incl_ones, la_lo)

        hs = range(heads)
        gam = [gam_all[:, h:h + 1] for h in hs]
        beta = [beta_all[:, B_HEADS + h:B_HEADS + h + 1] for h in hs]
        cols = [slice(h * dk, (h + 1) * dk) for h in hs]
        kh = [k_ref[0, rows, cols[h]] for h in hs]
        qh = [q_ref[0, rows, cols[h]] * (dk ** -0.5) for h in hs]
        vh = [v_ref[0, rows, cols[h]] for h in hs]
        kb = [kh[h].astype(BF16) for h in hs]
        kk = [_dot_nt(kb[h], kb[h]) for h in hs]
        qk = [_dot_nt(qh[h].astype(BF16), kb[h]) for h in hs]

        dec = []
        for h in hs:
            gcol = jnp.broadcast_to(gam[h], (c, c))
            diff = gcol - gcol.T
            dec.append(jnp.where(strict, jnp.exp(jnp.where(strict, diff, 0.0)), 0.0))
        nmat = [beta[h] * kk[h] * dec[h] for h in hs]
        a_qk = [(qk[h] * (dec[h] + eye)).astype(BF16) for h in hs]

        mp = [-(nmat[h] * diag_mask) for h in hs]
        x = [eye + mp[h] for h in hs]
        for _ in range(GDN_INV_BLOCK.bit_length() - 2):
            mpb = [mp[h].astype(BF16) for h in hs]
            mp = [_dot(mpb[h], mpb[h]) for h in hs]
            x = [x[h] + _dot(x[h].astype(BF16), mp[h].astype(BF16)) for h in hs]
        nb = [nmat[h].astype(BF16) for h in hs]
        for lm in level_masks:
            xb = [x[h].astype(BF16) for h in hs]
            cx = [_dot(nb[h] * lm, xb[h]).astype(BF16) for h in hs]
            x = [x[h] - _dot(xb[h], cx[h]) for h in hs]
        xb = [x[h].astype(BF16) for h in hs]

        g_cum = [jnp.exp(gam[h]) for h in hs]
        w = [_dot(xb[h], ((beta[h] * g_cum[h]) * kh[h]).astype(BF16)).astype(BF16) for h in hs]
        u = [_dot(xb[h], (beta[h] * vh[h]).astype(BF16)) for h in hs]
        g_last = [gam[h][c - 1:c, :] for h in hs]
        q_dec = [(qh[h] * g_cum[h]).astype(BF16) for h in hs]
        k_dec = [(kh[h] * jnp.exp(g_last[h] - gam[h])).astype(BF16) for h in hs]

        state = [st_ref[h] for h in hs]
        sb = [state[h].astype(BF16) for h in hs]
        db = [(u[h] - _dot(w[h], sb[h])).astype(BF16) for h in hs]
        out = [_dot(q_dec[h], sb[h]) + _dot(a_qk[h], db[h]) for h in hs]
        for h in hs:
            st_ref[h] = jnp.exp(g_last[h]) * state[h] + _dot_tn(k_dec[h], db[h])
        for h in hs:
            normed = (out[h] * lax.rsqrt(jnp.mean(out[h] * out[h], axis=-1, keepdims=True) + RMS_EPS)
                      * ng_ref[...])
            o_ref[0, rows, cols[h]] = (normed * _silu(z_ref[0, rows, cols[h]])).astype(BF16)
        return carry

    lax.fori_loop(0, n_steps, step, 0)


def _gdn(qkv, z, ab, a_log, dt_bias, norm_g):
    bsz, seq, _ = qkv.shape
    heads = B_HEADS
    gw = heads * B_KEY_DIM
    pad_lanes = lambda v: jnp.pad(v.reshape(1, -1), ((0, 0), (0, LANES - v.size)))
    kern = functools.partial(_gdn_kernel, n_steps=seq // GDN_CHUNK, heads=heads)
    col_block = lambda off: (lambda b: (b, 0, off))
    const2 = lambda b: (0, 0)
    return pl.pallas_call(
        kern,
        grid=(bsz,),
        in_specs=[pl.BlockSpec((1, seq, gw), col_block(0)),
                  pl.BlockSpec((1, seq, gw), col_block(1)),
                  pl.BlockSpec((1, seq, gw), col_block(2)),
                  pl.BlockSpec((1, seq, gw), col_block(0)),
                  pl.BlockSpec((1, seq, LANES), lambda b: (b, 0, 0)),
                  pl.BlockSpec((1, LANES), const2),
                  pl.BlockSpec((1, LANES), const2),
                  pl.BlockSpec((1, B_VAL_DIM), const2)],
        out_specs=pl.BlockSpec((1, seq, gw), col_block(0)),
        out_shape=jax.ShapeDtypeStruct((bsz, seq, B_HEADS * B_VAL_DIM), BF16),
        scratch_shapes=[pltpu.VMEM((heads, B_KEY_DIM, B_VAL_DIM), F32)],
        compiler_params=_params("arbitrary"),
        name="gdn",
    )(qkv, qkv, qkv, z, ab, pad_lanes(a_log), pad_lanes(dt_bias), norm_g.reshape(1, B_VAL_DIM))


def _outproj_kernel(x_ref, mod_ref, attn_ref, dn_ref, wo_ref, g_ref, b_ref, o_ref, *, n_out_chunks):
    gate = mod_ref[0, 5:6, :]
    rows_per = x_ref.shape[1] // n_out_chunks
    for r in range(n_out_chunks):
        rows = slice(r * rows_per, (r + 1) * rows_per)
        mixed = jnp.concatenate([attn_ref[0, rows, :], dn_ref[0, rows, :]], axis=1)
        y = DEEPNORM_ALPHA * x_ref[0, rows, :] + gate * _dot(mixed, wo_ref[...])
        o_ref[0, rows, :] = _layer_norm(y, g_ref[...], b_ref[...])


def _outproj(x, mod, attn, dn, w_out, ln_g, ln_b):
    bsz, seq, d = x.shape
    tm = min(OUT_TOKENS, seq)
    tok = lambda b, i: (b, i, 0)
    const2 = lambda b, i: (0, 0)
    kern = functools.partial(_outproj_kernel, n_out_chunks=max(1, tm // FFN_OUT_ROWS))
    return pl.pallas_call(
        kern,
        grid=(bsz, seq // tm),
        in_specs=[pl.BlockSpec((1, tm, d), tok),
                  pl.BlockSpec((1, N_MOD, d), lambda b, i: (b, 0, 0)),
                  pl.BlockSpec((1, tm, attn.shape[2]), tok),
                  pl.BlockSpec((1, tm, dn.shape[2]), tok),
                  pl.BlockSpec(w_out.shape, const2),
                  pl.BlockSpec((1, d), const2),
                  pl.BlockSpec((1, d), const2)],
        out_specs=pl.BlockSpec((1, tm, d), tok),
        out_shape=jax.ShapeDtypeStruct((bsz, seq, d), F32),
        compiler_params=_params("arbitrary", "arbitrary"),
        name="outproj",
    )(x, mod, attn, dn, w_out.astype(BF16), ln_g.reshape(1, d), ln_b.reshape(1, d))


def kernel(x, c, positions, w_ada, b_ada, ffn1_w1, ffn1_w3, ffn1_w2, ln1_g, ln1_b, w_in, conv_w, a_log, dt_bias, dn_norm_g, w_out, ln2_g, ln2_b, ffn2_w1, ffn2_w3, ffn2_w2, ln3_g, ln3_b):
    bsz, seq, d = x.shape
    pos_f = positions.astype(F32).reshape(bsz, 1, seq)
    for layer in range(w_ada.shape[0]):
        mod = _adaln(c, w_ada[layer], b_ada[layer]).reshape(bsz, N_MOD, d)
        x = _ffn_block(x, mod, ffn1_w1[layer], ffn1_w3[layer], ffn1_w2[layer], ln1_g[layer], ln1_b[layer],
                       mod_base=0)
        qt, qit, wt, k, ki, vt, qkv, z, ab = _proj(x, mod, pos_f, w_in[layer], conv_w[layer])
        attn = _dsa(qt, qit, wt, k, ki, vt)
        dn = _gdn(qkv, z, ab, a_log[layer], dt_bias[layer], dn_norm_g[layer])
        x = _outproj(x, mod, attn, dn, w_out[layer], ln2_g[layer], ln2_b[layer])
        x = _ffn_block(x, mod, ffn2_w1[layer], ffn2_w3[layer], ffn2_w2[layer], ln3_g[layer], ln3_b[layer],
                       mod_base=6)
    return x
```

```python
import functools

import jax
import jax.numpy as jnp
from jax import lax
from jax.experimental import pallas as pl
from jax.experimental.pallas import tpu as pltpu

F32 = jnp.float32
BF16 = jnp.bfloat16

A_HEADS = 8
A_HEAD_DIM = 64
IDX_HEADS = 8
IDX_DIM = 64
TOPK_MAX = 256
ROPE_THETA = 10000.0
B_HEADS = 4
B_KEY_DIM = 128
B_VAL_DIM = 128
CONV_WIDTH = 4
MACARON_WEIGHT = 0.5
DEPTH = 1
DEEPNORM_ALPHA = (2.0 * DEPTH) ** 0.25
LN_EPS = 1e-5
RMS_EPS = 1e-6
N_MOD = 9

LANES = 128
SUBLANES = 8
BF16_ROWS = 16
VMEM_LIMIT_BYTES = 56 * 1024 * 1024

FFN_TOKENS = 1024
FFN_CHUNK = 256
FFN_OUT_ROWS = 256
PROJ_TOKENS = 512
PROJ_CONV_COLS = 256
PROJ_CONV_ROWS = 128
OUT_TOKENS = 1024
OUT_ROWS = 256
ROW_CHUNK = 256
Q_BLOCK = 256
KEY_CHUNK = 256
GDN_CHUNK = 256
GDN_INV_BLOCK = 4

COUNT_ROWS = 32

NEG_INF = float("-inf")


def _silu(x):
    return x * (1.0 / (1.0 + jnp.exp(-x)))


def _dot(a, b):
    return jnp.dot(a, b, preferred_element_type=F32)


def _dot_nt(a, b):
    return lax.dot_general(a, b, (((1,), (1,)), ((), ())), preferred_element_type=F32)


def _dot_tn(a, b):
    return lax.dot_general(a, b, (((0,), (0,)), ((), ())), preferred_element_type=F32)


def _layer_norm(y, g, b):
    mu = jnp.mean(y, axis=-1, keepdims=True)
    d = y - mu
    var = jnp.mean(d * d, axis=-1, keepdims=True)
    return d * lax.rsqrt(var + LN_EPS) * g + b


def _params(*sem):
    return pltpu.CompilerParams(dimension_semantics=sem, vmem_limit_bytes=VMEM_LIMIT_BYTES)


def _adaln_kernel(c_ref, w_ref, b_ref, o_ref):
    sc = _silu(c_ref[...]).astype(BF16)
    o_ref[...] = _dot(sc, w_ref[...].astype(BF16)) + b_ref[...]


def _adaln(c, w, b):
    bsz, d = c.shape
    n = w.shape[1]
    tn = n // 8
    return pl.pallas_call(
        _adaln_kernel,
        grid=(n // tn,),
        in_specs=[pl.BlockSpec((bsz, d), lambda j: (0, 0)),
                  pl.BlockSpec((d, tn), lambda j: (0, j)),
                  pl.BlockSpec((1, tn), lambda j: (0, j))],
        out_specs=pl.BlockSpec((bsz, tn), lambda j: (0, j)),
        out_shape=jax.ShapeDtypeStruct((bsz, n), F32),
        compiler_params=_params("arbitrary"),
        name="adaln",
    )(c, w, b.reshape(1, n))


def _ffn_kernel(x_ref, mod_ref, w1_ref, w3_ref, w2_ref, g_ref, b_ref, o_ref, u_ref, hid_ref,
                *, mod_base, n_chunks, n_row_chunks, n_out_chunks):
    shift = mod_ref[0, mod_base:mod_base + 1, :]
    scale = mod_ref[0, mod_base + 1:mod_base + 2, :]
    gate = mod_ref[0, mod_base + 2:mod_base + 3, :]

    def prologue(r, carry):
        rows = pl.ds(pl.multiple_of(r * ROW_CHUNK, ROW_CHUNK), ROW_CHUNK)
        u_ref[rows, :] = (x_ref[0, rows, :] * (1.0 + scale) + shift).astype(BF16)
        return carry

    lax.fori_loop(0, n_row_chunks, prologue, 0)

    u = u_ref[...]
    for f in range(n_chunks):
        cols = slice(f * FFN_CHUNK, (f + 1) * FFN_CHUNK)
        a = _dot(u, w1_ref[:, cols])
        b = _dot(u, w3_ref[:, cols])
        hid_ref[:, cols] = (_silu(a) * b).astype(BF16)

    rows_per = x_ref.shape[1] // n_out_chunks
    for r in range(n_out_chunks):
        rows = slice(r * rows_per, (r + 1) * rows_per)
        h = _dot(hid_ref[rows, :], w2_ref[...])
        y =DEEPNORM_ALPHA * x_ref[0, rows, :] + (MACARON_WEIGHT * gate) * h
        o_ref[0, rows, :] = _layer_norm(y, g_ref[...], b_ref[...])


def _ffn_block(x, mod, w1, w3, w2, ln_g, ln_b, *, mod_base):
    bsz, seq, d = x.shape
    dff = w1.shape[1]
    n_chunks = dff // FFN_CHUNK
    tm = min(FFN_TOKENS, seq)
    kern = functools.partial(_ffn_kernel, mod_base=mod_base, n_chunks=n_chunks,
                             n_row_chunks=tm // ROW_CHUNK, n_out_chunks=max(1, tm // FFN_OUT_ROWS))
    const2 = lambda b, i: (0, 0)
    return pl.pallas_call(
        kern,
        grid=(bsz, seq // tm),
        in_specs=[pl.BlockSpec((1, tm, d), lambda b, i: (b, i, 0)),
                  pl.BlockSpec((1, N_MOD, d), lambda b, i: (b, 0, 0)),
                  pl.BlockSpec((d, dff), const2, pipeline_mode=pl.Buffered(1)),
                  pl.BlockSpec((d, dff), const2, pipeline_mode=pl.Buffered(1)),
                  pl.BlockSpec((dff, d), const2, pipeline_mode=pl.Buffered(1)),
                  pl.BlockSpec((1, d), lambda b, i: (0, 0)),
                  pl.BlockSpec((1, d), lambda b, i: (0, 0))],
        out_specs=pl.BlockSpec((1, tm, d), lambda b, i: (b, i, 0)),
        out_shape=jax.ShapeDtypeStruct((bsz, seq, d), F32),
        scratch_shapes=[pltpu.VMEM((tm, d), BF16), pltpu.VMEM((tm, dff), BF16)],
        compiler_params=_params("arbitrary", "arbitrary"),
        name="ffn",
    )(x, mod, w1.astype(BF16), w3.astype(BF16), w2.astype(BF16), ln_g.reshape(1, d), ln_b.reshape(1, d))


_ROW_Q = 0
_ROW_QI = _ROW_Q + A_HEADS * A_HEAD_DIM
_ROW_K = _ROW_QI + IDX_HEADS * IDX_DIM
_ROW_KI = _ROW_K + A_HEAD_DIM
_ROW_V = _ROW_KI + IDX_DIM
_ROW_W = _ROW_V + A_HEAD_DIM
_ROWS_A = _ROW_W + IDX_HEADS
_ROWS_A_PADDED = -(-_ROWS_A // 16) * 16
_COL_QKV = 0
_COL_Z = 2 * B_HEADS * B_KEY_DIM + B_HEADS * B_VAL_DIM
_COL_AB = _COL_Z + B_HEADS * B_VAL_DIM
_COLS_B = _COL_AB + LANES


def _proj_kernel(x_ref, mod_ref, pos_ref, invf_ref, wat_ref, wb_ref, cw_ref,
                 qt_ref, qit_ref, wt_ref, k_ref, ki_ref, vt_ref, qkv_ref, z_ref, ab_ref,
                 u_ref, pat_ref, prev_ref, *, n_row_chunks):
    shift = mod_ref[0, 3:4, :]
    scale = mod_ref[0, 4:5, :]
    tm = x_ref.shape[1]

    def prologue(r, carry):
        rows = pl.ds(pl.multiple_of(r * ROW_CHUNK, ROW_CHUNK), ROW_CHUNK)
        u_ref[rows, :] = (x_ref[0, rows, :] * (1.0 + scale) + shift).astype(BF16)
        return carry

    lax.fori_loop(0, n_row_chunks, prologue, 0)
    u = u_ref[...]

    halo = SUBLANES

    @pl.when(pl.program_id(1) == 0)
    def _():
        prev_ref[...] = jnp.zeros(prev_ref.shape, F32)

    n_normed = 2 * B_HEADS * B_KEY_DIM
    rb = PROJ_CONV_ROWS
    for c0 in range(_COL_QKV, _COL_Z, PROJ_CONV_COLS):
        cols = slice(c0, c0 + PROJ_CONV_COLS)
        prev = prev_ref[:, cols]
        for r0 in range(0, tm, rb):
            raw = _dot(u_ref[r0:r0 + rb, :], wb_ref[:, cols])
            xw = jnp.concatenate([prev, raw], axis=0)
            prev = raw[rb - halo:, :]
            y = raw * cw_ref[CONV_WIDTH - 1:CONV_WIDTH, cols]
            for j in range(CONV_WIDTH - 1):
                sh = CONV_WIDTH - 1 - j
                y = y + xw[halo - sh:halo - sh + rb] * cw_ref[j:j + 1, cols]
            y = _silu(y)
            for h0 in range(0, PROJ_CONV_COLS, B_KEY_DIM):
                t = y[:, h0:h0 + B_KEY_DIM]
                if c0 + h0 < n_normed:
                    t = t * lax.rsqrt(jnp.sum(t * t, axis=-1, keepdims=True) + RMS_EPS)
                qkv_ref[0, r0:r0 + rb, c0 + h0:c0 + h0 + B_KEY_DIM] = t
        prev_ref[:, cols] = prev
    z_ref[0] = _dot(u, wb_ref[:, _COL_Z:_COL_AB])
    ab_ref[0] = _dot(u, wb_ref[:, _COL_AB:_COLS_B])

    ang = invf_ref[...] * pos_ref[0]
    cos = jnp.cos(ang)
    sin = jnp.sin(ang)
    half = A_HEAD_DIM // 2

    pat_ref[...] = _dot_nt(wat_ref[...], u)

    def roped(row0):
        t1 = pat_ref[row0:row0 + half, :]
        t2 = pat_ref[row0 + half:row0 + A_HEAD_DIM, :]
        return t1 * cos - t2 * sin, t2 * cos + t1 * sin

    def store_heads(dst_ref, row_base, scale):
        for h in range(A_HEADS):
            o1, o2 = roped(row_base + h * A_HEAD_DIM)
            o1 = (o1 * scale).astype(BF16)
            o2 = (o2 * scale).astype(BF16)
            lanes = slice(h * Q_BLOCK, (h + 1) * Q_BLOCK)
            for j in range(dst_ref.shape[1]):
                toks = slice(j * Q_BLOCK, (j + 1) * Q_BLOCK)
                dst_ref[0, j, :half, lanes] = o1[:, toks]
                dst_ref[0, j, half:, lanes] = o2[:, toks]

    store_heads(qt_ref, _ROW_Q, A_HEAD_DIM ** -0.5)
    store_heads(qit_ref, _ROW_QI, 1.0)

    k1, k2 = roped(_ROW_K)
    ki1, ki2 = roped(_ROW_KI)
    kk = jnp.concatenate([k1, k2, ki1, ki2], axis=0).T
    k_ref[0] = kk[:, :A_HEAD_DIM].astype(BF16)
    ki_ref[0] = kk[:, A_HEAD_DIM:].astype(BF16)

    for j in range(vt_ref.shape[1]):
        vt_ref[0, j] = pat_ref[_ROW_V:_ROW_V + A_HEAD_DIM, j * KEY_CHUNK:(j + 1) * KEY_CHUNK].astype(BF16)

    wt_ref[0] = pat_ref[_ROW_W:_ROW_W + IDX_HEADS, :] * (IDX_HEADS ** -0.5 * IDX_DIM ** -0.5)


def _proj(x, mod, pos_f, w_in, conv_w):
    bsz, seq, d = x.shape
    tm = min(PROJ_TOKENS, seq)
    splits = (A_HEADS * A_HEAD_DIM, A_HEAD_DIM, A_HEAD_DIM, IDX_HEADS * IDX_DIM, IDX_DIM, IDX_HEADS,
              B_HEADS * B_KEY_DIM, B_HEADS * B_KEY_DIM, B_HEADS * B_VAL_DIM, B_HEADS * B_VAL_DIM,
              B_HEADS, B_HEADS)
    offs = [0]
    for s in splits:
        offs.append(offs[-1] + s)
    cols = [w_in[:, offs[i]:offs[i + 1]] for i in range(len(splits))]
    a_q, a_k, a_v, i_q, i_k, i_w, b_q, b_k, b_v, b_z, b_a, b_b = cols
    wat = jnp.concatenate([a_q, i_q, a_k, i_k, a_v, i_w], axis=1).T.astype(BF16)
    wat = jnp.pad(wat, ((0, _ROWS_A_PADDED - _ROWS_A), (0, 0)))
    wb = jnp.concatenate([b_q, b_k, b_v, b_z, b_a, b_b], axis=1).astype(BF16)
    wb = jnp.pad(wb, ((0, 0), (0, _COLS_B - wb.shape[1])))
    half = A_HEAD_DIM // 2
    inv_freq = (ROPE_THETA ** (-jnp.arange(0, A_HEAD_DIM, 2, dtype=F32) / A_HEAD_DIM)).reshape(half, 1)

    n_kc = tm // KEY_CHUNK
    qkv_w = _COL_Z
    z_w = _COL_AB - _COL_Z
    tok = lambda b, i: (b, i, 0)
    tok_t = lambda b, i: (b, 0, i)
    const2 = lambda b, i: (0, 0)
    n_qb = tm // Q_BLOCK
    q_spec = pl.BlockSpec((1, n_qb, A_HEAD_DIM, A_HEADS * Q_BLOCK), lambda b, i: (b, i, 0, 0))
    out_shapes = (
        jax.ShapeDtypeStruct((bsz, seq // Q_BLOCK, A_HEAD_DIM, A_HEADS * Q_BLOCK), BF16),
        jax.ShapeDtypeStruct((bsz, seq // Q_BLOCK, IDX_DIM, IDX_HEADS * Q_BLOCK), BF16),
        jax.ShapeDtypeStruct((bsz, IDX_HEADS, seq), F32),
        jax.ShapeDtypeStruct((bsz, seq, A_HEAD_DIM), BF16),
        jax.ShapeDtypeStruct((bsz, seq, IDX_DIM), BF16),
        jax.ShapeDtypeStruct((bsz, seq // KEY_CHUNK, A_HEAD_DIM, KEY_CHUNK), BF16),
        jax.ShapeDtypeStruct((bsz, seq, qkv_w), F32),
        jax.ShapeDtypeStruct((bsz, seq, z_w), F32),
        jax.ShapeDtypeStruct((bsz, seq, LANES), F32),
    )
    out_specs = (
        q_spec,
        q_spec,
        pl.BlockSpec((1, IDX_HEADS, tm), tok_t),
        pl.BlockSpec((1, tm, A_HEAD_DIM), tok),
        pl.BlockSpec((1, tm, IDX_DIM), tok),
        pl.BlockSpec((1, n_kc, A_HEAD_DIM, KEY_CHUNK), lambda b, i: (b, i, 0, 0)),
        pl.BlockSpec((1, tm, qkv_w), tok),
        pl.BlockSpec((1, tm, z_w), tok),
        pl.BlockSpec((1, tm, LANES), tok),
    )
    kern = functools.partial(_proj_kernel, n_row_chunks=tm // ROW_CHUNK)
    return pl.pallas_call(
        kern,
        grid=(bsz, seq // tm),
        in_specs=[pl.BlockSpec((1, tm, d), tok),
                  pl.BlockSpec((1, N_MOD, d), lambda b, i: (b, 0, 0)),
                  pl.BlockSpec((1, 1, tm), tok_t),
                  pl.BlockSpec((half, 1), const2),
                  pl.BlockSpec((_ROWS_A_PADDED, d), const2),
                  pl.BlockSpec((d, _COLS_B), const2),
                  pl.BlockSpec(conv_w.shape, const2)],
        out_specs=out_specs,
        out_shape=out_shapes,
        scratch_shapes=[pltpu.VMEM((tm, d), BF16), pltpu.VMEM((_ROWS_A_PADDED, tm), F32),
                        pltpu.VMEM((SUBLANES, qkv_w), F32)],
        compiler_params=_params("arbitrary", "arbitrary"),
        name="proj",
    )(x, mod, pos_f, inv_freq, wat, wb, conv_w)


_KEY_LO = -2139095041
_KEY_HI = 2139095041


def _key_to_float(key):
    bits = jnp.where(key < 0, key ^ 0x7FFFFFFF, key)
    return lax.bitcast_convert_type(bits, F32)


def _fold_rows(t):
    return t.reshape(t.shape[0] // SUBLANES, SUBLANES, t.shape[1])


def _dsa_kernel(qt_ref, qit_ref, wt_ref, k_ref, ki_ref, vt_ref, o_ref, s_ref, lg_ref, ot_ref,
                *, n_sel, search_steps):
    blk = pl.program_id(1)
    kc = KEY_CHUNK
    q_base = blk * Q_BLOCK
    n_keys = q_base + Q_BLOCK
    n_kc = (n_keys + kc - 1) // kc
    row = lax.broadcasted_iota(jnp.int32, (kc, Q_BLOCK), 0)
    col = lax.broadcasted_iota(jnp.int32, (kc, Q_BLOCK), 1)
    wt = wt_ref[0]
    heads = range(A_HEADS)
    lanes = [slice(h * Q_BLOCK, (h + 1) * Q_BLOCK) for h in heads]

    def chunk_rows(c):
        if isinstance(c, int):
            return slice(c * kc, (c + 1) * kc)
        return pl.ds(pl.multiple_of(c * kc, kc), kc)

    def per_chunk_count(make_variant):
        return lax.switch(n_kc - 1, [make_variant(n) for n in range(1, s_ref.shape[0] + 1)])

    def score_chunk(c):
        x = _dot(ki_ref[0, chunk_rows(c), :], qit_ref[0, 0])
        acc = wt[0:1, :] * jnp.maximum(x[:, lanes[0]], 0.0)
        for h in range(1, IDX_HEADS):
            acc = acc + wt[h:h + 1, :] * jnp.maximum(x[:, lanes[h]], 0.0)
        causal = row + (c * kc - q_base) <= col
        s_ref[c] = jnp.where(causal, acc, NEG_INF)

    def score_all(n):
        def run():
            for c in range(n):
                score_chunk(c)
            return 0
        return run

    per_chunk_count(score_all)

    def count(pred):
        def body(c, acc):
            ind = jnp.where(pred(s_ref[c]), 1.0, 0.0)
            return acc + jnp.sum(ind.reshape(kc // COUNT_ROWS, COUNT_ROWS, Q_BLOCK), axis=0)

        acc = lax.fori_loop(0, n_kc, body, jnp.zeros((COUNT_ROWS, Q_BLOCK), F32))
        return jnp.sum(acc, axis=0, keepdims=True)

    def rewrite_scores(fn):
        def body(c, carry):
            s_ref[c] = fn(s_ref[c])
            return carry
        lax.fori_loop(0, n_kc, body, 0)

    @pl.when(n_keys <= n_sel)
    def _():
        rewrite_scores(lambda s: jnp.where(s > NEG_INF, 0.0, NEG_INF))

    @pl.when(n_keys > n_sel)
    def _():
        lo0 = jnp.full((1, Q_BLOCK), _KEY_LO, jnp.int32)
        hi0 = jnp.full((1, Q_BLOCK), _KEY_HI, jnp.int32)
        cnt0 = jnp.full((1, Q_BLOCK), float(n_sel), F32)

        def search_over(n):
            def count_ge(thr):
                acc = jnp.zeros((COUNT_ROWS, Q_BLOCK), F32)
                for c in range(n):
                    ind = jnp.where(s_ref[c] >= thr, 1.0, 0.0)
                    acc = acc + jnp.sum(ind.reshape(kc // COUNT_ROWS, COUNT_ROWS, Q_BLOCK), axis=0)
                return jnp.sum(acc, axis=0, keepdims=True)

            def step(i, carry):
                lo, hi, cnt_lo = carry
                mid = (lo & hi) + ((lo ^ hi) >> 1)
                cnt = count_ge(_key_to_float(mid))
                ok = cnt >= n_sel
                return jnp.where(ok, mid, lo), jnp.where(ok, hi, mid), jnp.where(ok, cnt, cnt_lo)

            return lambda: lax.fori_loop(0, search_steps, step, (lo0, hi0, cnt0))

        lo, _, cnt_lo = lax.switch(n_kc - 1, [search_over(n) for n in range(1, s_ref.shape[0] + 1)])
        thr = _key_to_float(lo)
        has_ties = jnp.max(cnt_lo) > n_sel

        @pl.when(jnp.logical_not(has_ties))
        def _():
            rewrite_scores(lambda s: jnp.where(s >= thr, jnp.where(s > NEG_INF, 0.0, NEG_INF), NEG_INF))

        @pl.when(has_ties)
        def _():
            quota = n_sel - count(lambda s: s > thr)
            lower = (lax.broadcasted_iota(jnp.int32, (kc, kc), 1)
                     < lax.broadcasted_iota(jnp.int32, (kc, kc), 0))
            lower = jnp.where(lower, 1.0, 0.0).astype(BF16)

            def body(c, seen):
                s = s_ref[c]
                eq = jnp.where(s == thr, 1.0, 0.0)
                rank = _dot(lower, eq.astype(BF16)) + seen
                take = jnp.where(s > thr, 1.0, jnp.where(rank < quota, eq, 0.0))
                s_ref[c] = jnp.where(take > 0.5, jnp.where(s > NEG_INF, 0.0, NEG_INF), NEG_INF)
                return seen + jnp.sum(eq, axis=0, keepdims=True)

            lax.fori_loop(0, n_kc, body, jnp.zeros((1, Q_BLOCK), F32))

    def logits_chunk(c, m):
        bias = s_ref[c]
        lg = _dot(k_ref[0, chunk_rows(c), :], qt_ref[0, 0]) + jnp.concatenate([bias] * A_HEADS, axis=1)
        lg_ref[c] = lg
        return jnp.maximum(m, jnp.max(_fold_rows(lg), axis=0))

    def logits_all(n):
        def run():
            m = jnp.full((SUBLANES, A_HEADS * Q_BLOCK), NEG_INF, F32)
            for c in range(n):
                m = logits_chunk(c, m)
            return m
        return run

    m = jnp.max(per_chunk_count(logits_all), axis=0, keepdims=True)

    n_ones = ot_ref.shape[0] - A_HEAD_DIM

    def pv_all(n):
        def run():
            acc = None
            for c0 in range(0, n, 2):
                cs = range(c0, min(c0 + 2, n))
                lg = jnp.concatenate([lg_ref[c] for c in cs], axis=0)
                p = jnp.exp((lg - m).astype(BF16))
                vt_aug = jnp.concatenate([jnp.concatenate([vt_ref[0, c] for c in cs], axis=1),
                                          jnp.ones((n_ones, len(cs) * kc), BF16)], axis=0)
                part = _dot(vt_aug, p)
                acc = part if acc is None else acc + part
            ot_ref[...] = acc
            return 0
        return run

    per_chunk_count(pv_all)
    o = ot_ref[:A_HEAD_DIM, :] / ot_ref[A_HEAD_DIM:A_HEAD_DIM + 1, :]
    o = jnp.concatenate([o[:, lanes[h]] for h in heads], axis=0)
    o_ref[0] = o.T.astype(BF16)


def _dsa(qt, qit, wt, k, ki, vt):
    bsz, n_blk, _, hq = qt.shape
    seq = n_blk * Q_BLOCK
    n_sel = min(TOPK_MAX, seq // 4)
    n_kc = seq // KEY_CHUNK
    hd = A_HEADS * A_HEAD_DIM
    kern = functools.partial(_dsa_kernel, n_sel=n_sel, search_steps=32)
    return pl.pallas_call(
        kern,
        grid=(bsz, n_blk),
        in_specs=[pl.BlockSpec((1, 1, A_HEAD_DIM, hq), lambda b, i: (b, i, 0, 0)),
                  pl.BlockSpec((1, 1, IDX_DIM, hq), lambda b, i: (b, i, 0, 0)),
                  pl.BlockSpec((1, IDX_HEADS, Q_BLOCK), lambda b, i: (b, 0, i)),
                  pl.BlockSpec((1, seq, A_HEAD_DIM), lambda b, i: (b, 0, 0)),
                  pl.BlockSpec((1, seq, IDX_DIM), lambda b, i: (b, 0, 0)),
                  pl.BlockSpec((1, n_kc, A_HEAD_DIM, KEY_CHUNK), lambda b, i: (b, 0, 0, 0))],
        out_specs=pl.BlockSpec((1, Q_BLOCK, hd), lambda b, i: (b, i, 0)),
        out_shape=jax.ShapeDtypeStruct((bsz, seq, hd), BF16),
        scratch_shapes=[pltpu.VMEM((n_kc, KEY_CHUNK, Q_BLOCK), F32),
                        pltpu.VMEM((n_kc, KEY_CHUNK, hq), F32),
                        pltpu.VMEM((A_HEAD_DIM + BF16_ROWS, hq), F32)],
        compiler_params=_params("arbitrary", "arbitrary"),
        name="dsa",
    )(qt, qit, wt, k, ki, vt)


def _softplus(x):
    return jnp.maximum(x, 0.0) + jnp.log(1.0 + jnp.exp(-jnp.abs(x)))


def _split3(x):
    hi = x.astype(BF16)
    r = x - hi.astype(F32)
    mid = r.astype(BF16)
    lo = (r - mid.astype(F32)).astype(BF16)
    return hi, mid, lo


def _gdn_kernel(q_ref, k_ref, v_ref, z_ref, ab_ref, alog_ref, dtb_ref, ng_ref,
                o_ref, st_ref, *, n_steps, heads):
    c = GDN_CHUNK
    dk = B_KEY_DIM

    st_ref[...] = jnp.zeros(st_ref.shape, F32)
    ri = lax.broadcasted_iota(jnp.int32, (c, c), 0)
    ci = lax.broadcasted_iota(jnp.int32, (c, c), 1)
    strict = ri > ci
    eye = jnp.where(ri == ci, 1.0, 0.0)
    incl_ones = jnp.where(ri >= ci, 1.0, 0.0).astype(BF16)
    inv_shift = GDN_INV_BLOCK.bit_length() - 1
    diag_mask = jnp.where((ri >> inv_shift) == (ci >> inv_shift), 1.0, 0.0)
    level_masks = []
    for ls in range(inv_shift, c.bit_length() - 1):
        lower_left = jnp.where((ri >> ls) == (ci >> ls) + 1, 1.0, 0.0) * jnp.where(((ci >> ls) & 1) == 0, 1.0, 0.0)
        level_masks.append(lower_left.astype(BF16))

    def step(s, carry):
        rows = pl.ds(pl.multiple_of(s * c, c), c)
        ab = ab_ref[0, rows, :]
        log_a = -jnp.exp(alog_ref[...]) * _softplus(ab + dtb_ref[...])
        beta_all = 1.0 / (1.0 + jnp.exp(-ab))
        la_hi, la_mid, la_lo = _split3(log_a)
        gam_all = _dot(incl_ones, la_hi) + _dot(incl_ones, la_mid) + _dot(incl_ones, la_lo)

        hs = range(heads)
        gam = [gam_all[:, h:h + 1] for h in hs]
        beta = [beta_all[:, B_HEADS + h:B_HEADS + h + 1] for h in hs]
        cols = [slice(h * dk, (h + 1) * dk) for h in hs]
        kh = [k_ref[0, rows, cols[h]] for h in hs]
        qh = [q_ref[0, rows, cols[h]] * (dk ** -0.5) for h in hs]
        vh = [v_ref[0, rows, cols[h]] for h in hs]
        kb = [kh[h].astype(BF16) for h in hs]
        kk = [_dot_nt(kb[h], kb[h]) for h in hs]
        qk = [_dot_nt(qh[h].astype(BF16), kb[h]) for h in hs]

        dec = []
        for h in hs:
            gcol = jnp.broadcast_to(gam[h], (c, c))
            diff = gcol - gcol.T
            dec.append(jnp.where(strict, jnp.exp(jnp.where(strict, diff, 0.0)), 0.0))
        nmat = [beta[h] * kk[h] * dec[h] for h in hs]
        a_qk = [(qk[h] * (dec[h] + eye)).astype(BF16) for h in hs]

        mp = [-(nmat[h] * diag_mask) for h in hs]
        x = [eye + mp[h] for h in hs]
        for _ in range(GDN_INV_BLOCK.bit_length() - 2):
            mpb = [mp[h].astype(BF16) for h in hs]
            mp = [_dot(mpb[h], mpb[h]) for h in hs]
            x = [x[h] + _dot(x[h].astype(BF16), mp[h].astype(BF16)) for h in hs]
        nb = [nmat[h].astype(BF16) for h in hs]
        for lm in level_masks:
            xb = [x[h].astype(BF16) for h in hs]
            cx = [_dot(nb[h] * lm, xb[h]).astype(BF16) for h in hs]
            x = [x[h] - _dot(xb[h], cx[h]) for h in hs]
        xb = [x[h].astype(BF16) for h in hs]

        g_cum = [jnp.exp(gam[h]) for h in hs]
        w = [_dot(xb[h], ((beta[h] * g_cum[h]) * kh[h]).astype(BF16)).astype(BF16) for h in hs]
        u = [_dot(xb[h], (beta[h] * vh[h]).astype(BF16)) for h in hs]
        g_last = [gam[h][c - 1:c, :] for h in hs]
        q_dec = [(qh[h] * g_cum[h]).astype(BF16) for h in hs]
        k_dec = [(kh[h] * jnp.exp(g_last[h] - gam[h])).astype(BF16) for h in hs]

        state = [st_ref[h] for h in hs]
        sb = [state[h].astype(BF16) for h in hs]
        db = [(u[h] - _dot(w[h], sb[h])).astype(BF16) for h in hs]
        out = [_dot(q_dec[h], sb[h]) + _dot(a_qk[h], db[h]) for h in hs]
        for h in hs:
            st_ref[h] = jnp.exp(g_last[h]) * state[h] + _dot_tn(k_dec[h], db[h])
        for h in hs:
            normed = (out[h] * lax.rsqrt(jnp.mean(out[h] * out[h], axis=-1, keepdims=True) + RMS_EPS)
                      * ng_ref[...])
            o_ref[0, rows, cols[h]] = (normed * _silu(z_ref[0, rows, cols[h]])).astype(BF16)
        return carry

    lax.fori_loop(0, n_steps, step, 0)


def _gdn(qkv, z, ab, a_log, dt_bias, norm_g):
    bsz, seq, _ = qkv.shape
    heads = B_HEADS
    gw = heads * B_KEY_DIM
    pad_lanes = lambda v: jnp.pad(v.reshape(1, -1), ((0, 0), (0, LANES - v.size)))
    kern = functools.partial(_gdn_kernel, n_steps=seq // GDN_CHUNK, heads=heads)
    col_block = lambda off: (lambda b: (b, 0, off))
    const2 = lambda b: (0, 0)
    return pl.pallas_call(
        kern,
        grid=(bsz,),
        in_specs=[pl.BlockSpec((1, seq, gw), col_block(0)),
                  pl.BlockSpec((1, seq, gw), col_block(1)),
                  pl.BlockSpec((1, seq, gw), col_block(2)),
                  pl.BlockSpec((1, seq, gw), col_block(0)),
                  pl.BlockSpec((1, seq, LANES), lambda b: (b, 0, 0)),
                  pl.BlockSpec((1, LANES), const2),
                  pl.BlockSpec((1, LANES), const2),
                  pl.BlockSpec((1, B_VAL_DIM), const2)],
        out_specs=pl.BlockSpec((1, seq, gw), col_block(0)),
        out_shape=jax.ShapeDtypeStruct((bsz, seq, B_HEADS * B_VAL_DIM), BF16),
        scratch_shapes=[pltpu.VMEM((heads, B_KEY_DIM, B_VAL_DIM), F32)],
        compiler_params=_params("arbitrary"),
        name="gdn",
    )(qkv, qkv, qkv, z, ab, pad_lanes(a_log), pad_lanes(dt_bias), norm_g.reshape(1, B_VAL_DIM))


def _outproj_kernel(x_ref, mod_ref, attn_ref, dn_ref, wo_ref, g_ref, b_ref, o_ref, *, n_out_chunks):
    gate = mod_ref[0, 5:6, :]
    rows_per = x_ref.shape[1] // n_out_chunks
    for r in range(n_out_chunks):
        rows = slice(r * rows_per, (r + 1) * rows_per)
        mixed = jnp.concatenate([attn_ref[0, rows, :], dn_ref[0, rows, :]], axis=1)
        y = DEEPNORM_ALPHA * x_ref[0, rows, :] + gate * _dot(mixed, wo_ref[...])
        o_ref[0, rows, :] = _layer_norm(y, g_ref[...], b_ref[...])


def _outproj(x, mod, attn, dn, w_out, ln_g, ln_b):
    bsz, seq, d = x.shape
    tm = min(OUT_TOKENS, seq)
    tok = lambda b, i: (b, i, 0)
    const2 = lambda b, i: (0, 0)
    kern = functools.partial(_outproj_kernel, n_out_chunks=max(1, tm // OUT_ROWS))
    return pl.pallas_call(
        kern,
        grid=(bsz, seq // tm),
        in_specs=[pl.BlockSpec((1, tm, d), tok),
                  pl.BlockSpec((1, N_MOD, d), lambda b, i: (b, 0, 0)),
                  pl.BlockSpec((1, tm, attn.shape[2]), tok),
                  pl.BlockSpec((1, tm, dn.shape[2]), tok),
                  pl.BlockSpec(w_out.shape, const2),
                  pl.BlockSpec((1, d), const2),
                  pl.BlockSpec((1, d), const2)],
        out_specs=pl.BlockSpec((1, tm, d), tok),
        out_shape=jax.ShapeDtypeStruct((bsz, seq, d), F32),
        compiler_params=_params("arbitrary", "arbitrary"),
        name="outproj",
    )(x, mod, attn, dn, w_out.astype(BF16), ln_g.reshape(1, d), ln_b.reshape(1, d))


def kernel(x, c, positions, w_ada, b_ada, ffn1_w1, ffn1_w3, ffn1_w2, ln1_g, ln1_b, w_in, conv_w, a_log, dt_bias, dn_norm_g, w_out, ln2_g, ln2_b, ffn2_w1, ffn2_w3, ffn2_w2, ln3_g, ln3_b):
    bsz, seq, d = x.shape
    pos_f = positions.astype(F32).reshape(bsz, 1, seq)
    for layer in range(w_ada.shape[0]):
        mod = _adaln(c, w_ada[layer], b_ada[layer]).reshape(bsz, N_MOD, d)
        x = _ffn_block(x, mod, ffn1_w1[layer], ffn1_w3[layer], ffn1_w2[layer], ln1_g[layer], ln1_b[layer],
                       mod_base=0)
        qt, qit, wt, k, ki, vt, qkv, z, ab = _proj(x, mod, pos_f, w_in[layer], conv_w[layer])
        attn = _dsa(qt, qit, wt, k, ki, vt)
        dn = _gdn(qkv, z, ab, a_log[layer], dt_bias[layer], dn_norm_g[layer])
        x = _outproj(x, mod, attn, dn, w_out[layer], ln2_g[layer], ln2_b[layer])
        x = _ffn_block(x, mod, ffn2_w1[layer], ffn2_w3[layer], ffn2_w2[layer], ln3_g[layer], ln3_b[layer],
                       mod_base=6)
    return x
```

```python
import functools

import jax
import jax.numpy as jnp
from jax import lax
from jax.experimental import pallas as pl
from jax.experimental.pallas import tpu as pltpu

F32 = jnp.float32
BF16 = jnp.bfloat16

A_HEADS = 8
A_HEAD_DIM = 64
IDX_HEADS = 8
IDX_DIM = 64
TOPK_MAX = 256
ROPE_THETA = 10000.0
B_HEADS = 4
B_KEY_DIM = 128
B_VAL_DIM = 128
CONV_WIDTH = 4
MACARON_WEIGHT = 0.5
DEPTH = 1
DEEPNORM_ALPHA = (2.0 * DEPTH) ** 0.25
LN_EPS = 1e-5
RMS_EPS = 1e-6
N_MOD = 9

LANES = 128
SUBLANES = 8
BF16_ROWS = 16
VMEM_LIMIT_BYTES = 56 * 1024 * 1024

FFN_TOKENS = 1024
FFN_CHUNK = 256
FFN_OUT_ROWS = 256
PROJ_TOKENS = 1024
PROJ_CONV_COLS = 256
PROJ_CONV_ROWS = 128
OUT_TOKENS = 1024
OUT_ROWS = 256
ROW_CHUNK = 256
Q_BLOCK = 256
KEY_CHUNK = 256
GDN_CHUNK = 256
GDN_INV_BLOCK = 4

COUNT_ROWS = 32

NEG_INF = float("-inf")


def _silu(x):
    return x * (1.0 / (1.0 + jnp.exp(-x)))


def _dot(a, b):
    return jnp.dot(a, b, preferred_element_type=F32)


def _dot_nt(a, b):
    return lax.dot_general(a, b, (((1,), (1,)), ((), ())), preferred_element_type=F32)


def _dot_tn(a, b):
    return lax.dot_general(a, b, (((0,), (0,)), ((), ())), preferred_element_type=F32)


def _layer_norm(y, g, b):
    mu = jnp.mean(y, axis=-1, keepdims=True)
    d = y - mu
    var = jnp.mean(d * d, axis=-1, keepdims=True)
    return d * lax.rsqrt(var + LN_EPS) * g + b


def _params(*sem):
    return pltpu.CompilerParams(dimension_semantics=sem, vmem_limit_bytes=VMEM_LIMIT_BYTES)


def _adaln_kernel(c_ref, w_ref, b_ref, o_ref):
    sc = _silu(c_ref[...]).astype(BF16)
    o_ref[...] = _dot(sc, w_ref[...].astype(BF16)) + b_ref[...]


def _adaln(c, w, b):
    bsz, d = c.shape
    n = w.shape[1]
    tn = n // 8
    return pl.pallas_call(
        _adaln_kernel,
        grid=(n // tn,),
        in_specs=[pl.BlockSpec((bsz, d), lambda j: (0, 0)),
                  pl.BlockSpec((d, tn), lambda j: (0, j)),
                  pl.BlockSpec((1, tn), lambda j: (0, j))],
        out_specs=pl.BlockSpec((bsz, tn), lambda j: (0, j)),
        out_shape=jax.ShapeDtypeStruct((bsz, n), F32),
        compiler_params=_params("arbitrary"),
        name="adaln",
    )(c, w, b.reshape(1, n))


def _ffn_kernel(x_ref, mod_ref, w1_ref, w3_ref, w2_ref, g_ref, b_ref, o_ref, u_ref, hid_ref,
                *, mod_base, n_chunks, n_row_chunks, n_out_chunks):
    shift = mod_ref[0, mod_base:mod_base + 1, :]
    scale = mod_ref[0, mod_base + 1:mod_base + 2, :]
    gate = mod_ref[0, mod_base + 2:mod_base + 3, :]

    def prologue(r, carry):
        rows = pl.ds(pl.multiple_of(r * ROW_CHUNK, ROW_CHUNK), ROW_CHUNK)
        u_ref[rows, :] = (x_ref[0, rows, :] * (1.0 + scale) + shift).astype(BF16)
        return carry

    lax.fori_loop(0, n_row_chunks, prologue, 0)

    u = u_ref[...]
    for f in range(n_chunks):
        cols = slice(f * FFN_CHUNK, (f + 1) * FFN_CHUNK)
        a = _dot(u, w1_ref[:, cols])
        b = _dot(u, w3_ref[:, cols])
        hid_ref[:, cols] = (_silu(a) * b).astype(BF16)

    rows_per = x_ref.shape[1] // n_out_chunks
    for r in range(n_out_chunks):
        rows = slice(r * rows_per, (r + 1) * rows_per)
        h = _dot(hid_ref[rows, :], w2_ref[...])
        y =DEEPNORM_ALPHA * x_ref[0, rows, :] + (MACARON_WEIGHT * gate) * h
        o_ref[0, rows, :] = _layer_norm(y, g_ref[...], b_ref[...])


def _ffn_block(x, mod, w1, w3, w2, ln_g, ln_b, *, mod_base):
    bsz, seq, d = x.shape
    dff = w1.shape[1]
    n_chunks = dff // FFN_CHUNK
    tm = min(FFN_TOKENS, seq)
    kern = functools.partial(_ffn_kernel, mod_base=mod_base, n_chunks=n_chunks,
                             n_row_chunks=tm // ROW_CHUNK, n_out_chunks=max(1, tm // FFN_OUT_ROWS))
    const2 = lambda b, i: (0, 0)
    return pl.pallas_call(
        kern,
        grid=(bsz, seq // tm),
        in_specs=[pl.BlockSpec((1, tm, d), lambda b, i: (b, i, 0)),
                  pl.BlockSpec((1, N_MOD, d), lambda b, i: (b, 0, 0)),
                  pl.BlockSpec((d, dff), const2, pipeline_mode=pl.Buffered(1)),
                  pl.BlockSpec((d, dff), const2, pipeline_mode=pl.Buffered(1)),
                  pl.BlockSpec((dff, d), const2, pipeline_mode=pl.Buffered(1)),
                  pl.BlockSpec((1, d), lambda b, i: (0, 0)),
                  pl.BlockSpec((1, d), lambda b, i: (0, 0))],
        out_specs=pl.BlockSpec((1, tm, d), lambda b, i: (b, i, 0)),
        out_shape=jax.ShapeDtypeStruct((bsz, seq, d), F32),
        scratch_shapes=[pltpu.VMEM((tm, d), BF16), pltpu.VMEM((tm, dff), BF16)],
        compiler_params=_params("arbitrary", "arbitrary"),
        name="ffn",
    )(x, mod, w1.astype(BF16), w3.astype(BF16), w2.astype(BF16), ln_g.reshape(1, d), ln_b.reshape(1, d))


_ROW_Q = 0
_ROW_QI = _ROW_Q + A_HEADS * A_HEAD_DIM
_ROW_K = _ROW_QI + IDX_HEADS * IDX_DIM
_ROW_KI = _ROW_K + A_HEAD_DIM
_ROW_V = _ROW_KI + IDX_DIM
_ROW_W = _ROW_V + A_HEAD_DIM
_ROWS_A = _ROW_W + IDX_HEADS
_ROWS_A_PADDED = -(-_ROWS_A // 16) * 16
_COL_QKV = 0
_COL_Z = 2 * B_HEADS * B_KEY_DIM + B_HEADS * B_VAL_DIM
_COL_AB = _COL_Z + B_HEADS * B_VAL_DIM
_COLS_B = _COL_AB + LANES


def _proj_kernel(x_ref, mod_ref, pos_ref, invf_ref, wat_ref, wb_ref, cw_ref,
                 qt_ref, qit_ref, wt_ref, k_ref, ki_ref, vt_ref, qkv_ref, z_ref, ab_ref,
                 u_ref, pat_ref, prev_ref, *, n_row_chunks):
    shift = mod_ref[0, 3:4, :]
    scale = mod_ref[0, 4:5, :]
    tm = x_ref.shape[1]

    def prologue(r, carry):
        rows = pl.ds(pl.multiple_of(r * ROW_CHUNK, ROW_CHUNK), ROW_CHUNK)
        u_ref[rows, :] = (x_ref[0, rows, :] * (1.0 + scale) + shift).astype(BF16)
        return carry

    lax.fori_loop(0, n_row_chunks, prologue, 0)
    u = u_ref[...]

    halo = SUBLANES

    @pl.when(pl.program_id(1) == 0)
    def _():
        prev_ref[...] = jnp.zeros(prev_ref.shape, F32)

    n_normed = 2 * B_HEADS * B_KEY_DIM
    rb = PROJ_CONV_ROWS
    for c0 in range(_COL_QKV, _COL_Z, PROJ_CONV_COLS):
        cols = slice(c0, c0 + PROJ_CONV_COLS)
        prev = prev_ref[:, cols]
        for r0 in range(0, tm, rb):
            raw = _dot(u_ref[r0:r0 + rb, :], wb_ref[:, cols])
            xw = jnp.concatenate([prev, raw], axis=0)
            prev = raw[rb - halo:, :]
            y = raw * cw_ref[CONV_WIDTH - 1:CONV_WIDTH, cols]
            for j in range(CONV_WIDTH - 1):
                sh = CONV_WIDTH - 1 - j
                y = y + xw[halo - sh:halo - sh + rb] * cw_ref[j:j + 1, cols]
            y = _silu(y)
            for h0 in range(0, PROJ_CONV_COLS, B_KEY_DIM):
                t = y[:, h0:h0 + B_KEY_DIM]
                if c0 + h0 < n_normed:
                    t = t * lax.rsqrt(jnp.sum(t * t, axis=-1, keepdims=True) + RMS_EPS)
                qkv_ref[0, r0:r0 + rb, c0 + h0:c0 + h0 + B_KEY_DIM] = t
        prev_ref[:, cols] = prev
    z_ref[0] = _dot(u, wb_ref[:, _COL_Z:_COL_AB])
    ab_ref[0] = _dot(u, wb_ref[:, _COL_AB:_COLS_B])

    ang = invf_ref[...] * pos_ref[0]
    cos = jnp.cos(ang)
    sin = jnp.sin(ang)
    half = A_HEAD_DIM // 2

    pat_ref[...] = _dot_nt(wat_ref[...], u)

    def roped(row0):
        t1 = pat_ref[row0:row0 + half, :]
        t2 = pat_ref[row0 + half:row0 + A_HEAD_DIM, :]
        return t1 * cos - t2 * sin, t2 * cos + t1 * sin

    def store_heads(dst_ref, row_base, scale):
        for h in range(A_HEADS):
            o1, o2 = roped(row_base + h * A_HEAD_DIM)
            o1 = (o1 * scale).astype(BF16)
            o2 = (o2 * scale).astype(BF16)
            lanes = slice(h * Q_BLOCK, (h + 1) * Q_BLOCK)
            for j in range(dst_ref.shape[1]):
                toks = slice(j * Q_BLOCK, (j + 1) * Q_BLOCK)
                dst_ref[0, j, :half, lanes] = o1[:, toks]
                dst_ref[0, j, half:, lanes] = o2[:, toks]

    store_heads(qt_ref, _ROW_Q, A_HEAD_DIM ** -0.5)
    store_heads(qit_ref, _ROW_QI, 1.0)

    k1, k2 = roped(_ROW_K)
    ki1, ki2 = roped(_ROW_KI)
    kk = jnp.concatenate([k1, k2, ki1, ki2], axis=0).T
    k_ref[0] = kk[:, :A_HEAD_DIM].astype(BF16)
    ki_ref[0] = kk[:, A_HEAD_DIM:].astype(BF16)

    for j in range(vt_ref.shape[1]):
        vt_ref[0, j] = pat_ref[_ROW_V:_ROW_V + A_HEAD_DIM, j * KEY_CHUNK:(j + 1) * KEY_CHUNK].astype(BF16)

    wt_ref[0] = pat_ref[_ROW_W:_ROW_W + IDX_HEADS, :] * (IDX_HEADS ** -0.5 * IDX_DIM ** -0.5)


def _proj(x, mod, pos_f, w_in, conv_w):
    bsz, seq, d = x.shape
    tm = min(PROJ_TOKENS, seq)
    splits = (A_HEADS * A_HEAD_DIM, A_HEAD_DIM, A_HEAD_DIM, IDX_HEADS * IDX_DIM, IDX_DIM, IDX_HEADS,
              B_HEADS * B_KEY_DIM, B_HEADS * B_KEY_DIM, B_HEADS * B_VAL_DIM, B_HEADS * B_VAL_DIM,
              B_HEADS, B_HEADS)
    offs = [0]
    for s in splits:
        offs.append(offs[-1] + s)
    cols = [w_in[:, offs[i]:offs[i + 1]] for i in range(len(splits))]
    a_q, a_k, a_v, i_q, i_k, i_w, b_q, b_k, b_v, b_z, b_a, b_b = cols
    wat = jnp.concatenate([a_q, i_q, a_k, i_k, a_v, i_w], axis=1).T.astype(BF16)
    wat = jnp.pad(wat, ((0, _ROWS_A_PADDED - _ROWS_A), (0, 0)))
    wb = jnp.concatenate([b_q, b_k, b_v, b_z, b_a, b_b], axis=1).astype(BF16)
    wb = jnp.pad(wb, ((0, 0), (0, _COLS_B - wb.shape[1])))
    half = A_HEAD_DIM // 2
    inv_freq = (ROPE_THETA ** (-jnp.arange(0, A_HEAD_DIM, 2, dtype=F32) / A_HEAD_DIM)).reshape(half, 1)

    n_kc = tm // KEY_CHUNK
    qkv_w = _COL_Z
    z_w = _COL_AB - _COL_Z
    tok = lambda b, i: (b, i, 0)
    tok_t = lambda b, i: (b, 0, i)
    const2 = lambda b, i: (0, 0)
    n_qb = tm // Q_BLOCK
    q_spec = pl.BlockSpec((1, n_qb, A_HEAD_DIM, A_HEADS * Q_BLOCK), lambda b, i: (b, i, 0, 0))
    out_shapes = (
        jax.ShapeDtypeStruct((bsz, seq // Q_BLOCK, A_HEAD_DIM, A_HEADS * Q_BLOCK), BF16),
        jax.ShapeDtypeStruct((bsz, seq // Q_BLOCK, IDX_DIM, IDX_HEADS * Q_BLOCK), BF16),
        jax.ShapeDtypeStruct((bsz, IDX_HEADS, seq), F32),
        jax.ShapeDtypeStruct((bsz, seq, A_HEAD_DIM), BF16),
        jax.ShapeDtypeStruct((bsz, seq, IDX_DIM), BF16),
        jax.ShapeDtypeStruct((bsz, seq // KEY_CHUNK, A_HEAD_DIM, KEY_CHUNK), BF16),
        jax.ShapeDtypeStruct((bsz, seq, qkv_w), F32),
        jax.ShapeDtypeStruct((bsz, seq, z_w), F32),
        jax.ShapeDtypeStruct((bsz, seq, LANES), F32),
    )
    out_specs = (
        q_spec,
        q_spec,
        pl.BlockSpec((1, IDX_HEADS, tm), tok_t),
        pl.BlockSpec((1, tm, A_HEAD_DIM), tok),
        pl.BlockSpec((1, tm, IDX_DIM), tok),
        pl.BlockSpec((1, n_kc, A_HEAD_DIM, KEY_CHUNK), lambda b, i: (b, i, 0, 0)),
        pl.BlockSpec((1, tm, qkv_w), tok),
        pl.BlockSpec((1, tm, z_w), tok),
        pl.BlockSpec((1, tm, LANES), tok),
    )
    kern = functools.partial(_proj_kernel, n_row_chunks=tm // ROW_CHUNK)
    return pl.pallas_call(
        kern,
        grid=(bsz, seq // tm),
        in_specs=[pl.BlockSpec((1, tm, d), tok),
                  pl.BlockSpec((1, N_MOD, d), lambda b, i: (b, 0, 0)),
                  pl.BlockSpec((1, 1, tm), tok_t),
                  pl.BlockSpec((half, 1), const2),
                  pl.BlockSpec((_ROWS_A_PADDED, d), const2),
                  pl.BlockSpec((d, _COLS_B), const2),
                  pl.BlockSpec(conv_w.shape, const2)],
        out_specs=out_specs,
        out_shape=out_shapes,
        scratch_shapes=[pltpu.VMEM((tm, d), BF16), pltpu.VMEM((_ROWS_A_PADDED, tm), F32),
                        pltpu.VMEM((SUBLANES, qkv_w), F32)],
        compiler_params=_params("arbitrary", "arbitrary"),
        name="proj",
    )(x, mod, pos_f, inv_freq, wat, wb, conv_w)


_KEY_LO = -2139095041
_KEY_HI = 2139095041


def _key_to_float(key):
    bits = jnp.where(key < 0, key ^ 0x7FFFFFFF, key)
    return lax.bitcast_convert_type(bits, F32)


def _fold_rows(t):
    return t.reshape(t.shape[0] // SUBLANES, SUBLANES, t.shape[1])


def _dsa_kernel(qt_ref, qit_ref, wt_ref, k_ref, ki_ref, vt_ref, o_ref, s_ref, lg_ref, ot_ref,
                *, n_sel, search_steps):
    blk = pl.program_id(1)
    kc = KEY_CHUNK
    q_base = blk * Q_BLOCK
    n_keys = q_base + Q_BLOCK
    n_kc = (n_keys + kc - 1) // kc
    row = lax.broadcasted_iota(jnp.int32, (kc, Q_BLOCK), 0)
    col = lax.broadcasted_iota(jnp.int32, (kc, Q_BLOCK), 1)
    wt = wt_ref[0]
    heads = range(A_HEADS)
    lanes = [slice(h * Q_BLOCK, (h + 1) * Q_BLOCK) for h in heads]

    def chunk_rows(c):
        if isinstance(c, int):
            return slice(c * kc, (c + 1) * kc)
        return pl.ds(pl.multiple_of(c * kc, kc), kc)

    def per_chunk_count(make_variant):
        return lax.switch(n_kc - 1, [make_variant(n) for n in range(1, s_ref.shape[0] + 1)])

    def score_chunk(c):
        x = _dot(ki_ref[0, chunk_rows(c), :], qit_ref[0, 0])
        acc = wt[0:1, :] * jnp.maximum(x[:, lanes[0]], 0.0)
        for h in range(1, IDX_HEADS):
            acc = acc + wt[h:h + 1, :] * jnp.maximum(x[:, lanes[h]], 0.0)
        causal = row + (c * kc - q_base) <= col
        s_ref[c] = jnp.where(causal, acc, NEG_INF)

    def score_all(n):
        def run():
            for c in range(n):
                score_chunk(c)
            return 0
        return run

    per_chunk_count(score_all)

    def count(pred):
        def body(c, acc):
            ind = jnp.where(pred(s_ref[c]), 1.0, 0.0)
            return acc + jnp.sum(ind.reshape(kc // COUNT_ROWS, COUNT_ROWS, Q_BLOCK), axis=0)

        acc = lax.fori_loop(0, n_kc, body, jnp.zeros((COUNT_ROWS, Q_BLOCK), F32))
        return jnp.sum(acc, axis=0, keepdims=True)

    def rewrite_scores(fn):
        def body(c, carry):
            s_ref[c] = fn(s_ref[c])
            return carry
        lax.fori_loop(0, n_kc, body, 0)

    @pl.when(n_keys <= n_sel)
    def _():
        rewrite_scores(lambda s: jnp.where(s > NEG_INF, 0.0, NEG_INF))

    @pl.when(n_keys > n_sel)
    def _():
        lo0 = jnp.full((1, Q_BLOCK), _KEY_LO, jnp.int32)
        hi0 = jnp.full((1, Q_BLOCK), _KEY_HI, jnp.int32)
        cnt0 = jnp.full((1, Q_BLOCK), float(n_sel), F32)

        def search_over(n):
            def count_ge(thr):
                acc = jnp.zeros((COUNT_ROWS, Q_BLOCK), F32)
                for c in range(n):
                    ind = jnp.where(s_ref[c] >= thr, 1.0, 0.0)
                    acc = acc + jnp.sum(ind.reshape(kc // COUNT_ROWS, COUNT_ROWS, Q_BLOCK), axis=0)
                return jnp.sum(acc, axis=0, keepdims=True)

            def step(i, carry):
                lo, hi, cnt_lo = carry
                mid = (lo & hi) + ((lo ^ hi) >> 1)
                cnt = count_ge(_key_to_float(mid))
                ok = cnt >= n_sel
                return jnp.where(ok, mid, lo), jnp.where(ok, hi, mid), jnp.where(ok, cnt, cnt_lo)

            return lambda: lax.fori_loop(0, search_steps, step, (lo0, hi0, cnt0))

        lo, _, cnt_lo = lax.switch(n_kc - 1, [search_over(n) for n in range(1, s_ref.shape[0] + 1)])
        thr = _key_to_float(lo)
        has_ties = jnp.max(cnt_lo) > n_sel

        @pl.when(jnp.logical_not(has_ties))
        def _():
            rewrite_scores(lambda s: jnp.where(s >= thr, jnp.where(s > NEG_INF, 0.0, NEG_INF), NEG_INF))

        @pl.when(has_ties)
        def _():
            quota = n_sel - count(lambda s: s > thr)
            lower = (lax.broadcasted_iota(jnp.int32, (kc, kc), 1)
                     < lax.broadcasted_iota(jnp.int32, (kc, kc), 0))
            lower = jnp.where(lower, 1.0, 0.0).astype(BF16)

            def body(c, seen):
                s = s_ref[c]
                eq = jnp.where(s == thr, 1.0, 0.0)
                rank = _dot(lower, eq.astype(BF16)) + seen
                take = jnp.where(s > thr, 1.0, jnp.where(rank < quota, eq, 0.0))
                s_ref[c] = jnp.where(take > 0.5, jnp.where(s > NEG_INF, 0.0, NEG_INF), NEG_INF)
                return seen + jnp.sum(eq, axis=0, keepdims=True)

            lax.fori_loop(0, n_kc, body, jnp.zeros((1, Q_BLOCK), F32))

    def logits_chunk(c, m):
        bias = s_ref[c]
        lg = _dot(k_ref[0, chunk_rows(c), :], qt_ref[0, 0]) + jnp.concatenate([bias] * A_HEADS, axis=1)
        lg_ref[c] = lg
        return jnp.maximum(m, jnp.max(_fold_rows(lg), axis=0))

    def logits_all(n):
        def run():
            m = jnp.full((SUBLANES, A_HEADS * Q_BLOCK), NEG_INF, F32)
            for c in range(n):
                m = logits_chunk(c, m)
            return m
        return run

    m = jnp.max(per_chunk_count(logits_all), axis=0, keepdims=True)

    n_ones = ot_ref.shape[0] - A_HEAD_DIM

    def pv_all(n):
        def run():
            acc = None
            for c0 in range(0, n, 2):
                cs = range(c0, min(c0 + 2, n))
                lg = jnp.concatenate([lg_ref[c] for c in cs], axis=0)
                p = jnp.exp((lg - m).astype(BF16))
                vt_aug = jnp.concatenate([jnp.concatenate([vt_ref[0, c] for c in cs], axis=1),
                                          jnp.ones((n_ones, len(cs) * kc), BF16)], axis=0)
                part = _dot(vt_aug, p)
                acc = part if acc is None else acc + part
            ot_ref[...] = acc
            return 0
        return run

    per_chunk_count(pv_all)
    o = ot_ref[:A_HEAD_DIM, :] / ot_ref[A_HEAD_DIM:A_HEAD_DIM + 1, :]
    o = jnp.concatenate([o[:, lanes[h]] for h in heads], axis=0)
    o_ref[0] = o.T.astype(BF16)


def _dsa(qt, qit, wt, k, ki, vt):
    bsz, n_blk, _, hq = qt.shape
    seq = n_blk * Q_BLOCK
    n_sel = min(TOPK_MAX, seq // 4)
    n_kc = seq // KEY_CHUNK
    hd = A_HEADS * A_HEAD_DIM
    kern = functools.partial(_dsa_kernel, n_sel=n_sel, search_steps=32)
    return pl.pallas_call(
        kern,
        grid=(bsz, n_blk),
        in_specs=[pl.BlockSpec((1, 1, A_HEAD_DIM, hq), lambda b, i: (b, i, 0, 0)),
                  pl.BlockSpec((1, 1, IDX_DIM, hq), lambda b, i: (b, i, 0, 0)),
                  pl.BlockSpec((1, IDX_HEADS, Q_BLOCK), lambda b, i: (b, 0, i)),
                  pl.BlockSpec((1, seq, A_HEAD_DIM), lambda b, i: (b, 0, 0)),
                  pl.BlockSpec((1, seq, IDX_DIM), lambda b, i: (b, 0, 0)),
                  pl.BlockSpec((1, n_kc, A_HEAD_DIM, KEY_CHUNK), lambda b, i: (b, 0, 0, 0))],
        out_specs=pl.BlockSpec((1, Q_BLOCK, hd), lambda b, i: (b, i, 0)),
        out_shape=jax.ShapeDtypeStruct((bsz, seq, hd), BF16),
        scratch_shapes=[pltpu.VMEM((n_kc, KEY_CHUNK, Q_BLOCK), F32),
                        pltpu.VMEM((n_kc, KEY_CHUNK, hq), F32),
                        pltpu.VMEM((A_HEAD_DIM + BF16_ROWS, hq), F32)],
        compiler_params=_params("arbitrary", "arbitrary"),
        name="dsa",
    )(qt, qit, wt, k, ki, vt)


def _softplus(x):
    return jnp.maximum(x, 0.0) + jnp.log(1.0 + jnp.exp(-jnp.abs(x)))


def _split3(x):
    hi = x.astype(BF16)
    r = x - hi.astype(F32)
    mid = r.astype(BF16)
    lo = (r - mid.astype(F32)).astype(BF16)
    return hi, mid, lo


def _gdn_kernel(q_ref, k_ref, v_ref, z_ref, ab_ref, alog_ref, dtb_ref, ng_ref,
                o_ref, st_ref, *, n_steps, heads):
    c = GDN_CHUNK
    dk = B_KEY_DIM

    st_ref[...] = jnp.zeros(st_ref.shape, F32)
    ri = lax.broadcasted_iota(jnp.int32, (c, c), 0)
    ci = lax.broadcasted_iota(jnp.int32, (c, c), 1)
    strict = ri > ci
    eye = jnp.where(ri == ci, 1.0, 0.0)
    incl_ones = jnp.where(ri >= ci, 1.0, 0.0).astype(BF16)
    inv_shift = GDN_INV_BLOCK.bit_length() - 1
    diag_mask = jnp.where((ri >> inv_shift) == (ci >> inv_shift), 1.0, 0.0)
    level_masks = []
    for ls in range(inv_shift, c.bit_length() - 1):
        lower_left = jnp.where((ri >> ls) == (ci >> ls) + 1, 1.0, 0.0) * jnp.where(((ci >> ls) & 1) == 0, 1.0, 0.0)
        level_masks.append(lower_left.astype(BF16))

    def step(s, carry):
        rows = pl.ds(pl.multiple_of(s * c, c), c)
        ab = ab_ref[0, rows, :]
        log_a = -jnp.exp(alog_ref[...]) * _softplus(ab + dtb_ref[...])
        beta_all = 1.0 / (1.0 + jnp.exp(-ab))
        la_hi, la_mid, la_lo = _split3(log_a)
        gam_all = _dot(incl_ones, la_hi) + _dot(incl_ones, la_mid) + _dot(incl_ones, la_lo)

        hs = range(heads)
        gam = [gam_all[:, h:h + 1] for h in hs]
        beta = [beta_all[:, B_HEADS + h:B_HEADS + h + 1] for h in hs]
        cols = [slice(h * dk, (h + 1) * dk) for h in hs]
        kh = [k_ref[0, rows, cols[h]] for h in hs]
        qh = [q_ref[0, rows, cols[h]] * (dk ** -0.5) for h in hs]
        vh = [v_ref[0, rows, cols[h]] for h in hs]
        kb = [kh[h].astype(BF16) for h in hs]
        kk = [_dot_nt(kb[h], kb[h]) for h in hs]
        qk = [_dot_nt(qh[h].astype(BF16), kb[h]) for h in hs]

        dec = []
        for h in hs:
            gcol = jnp.broadcast_to(gam[h], (c, c))
            diff = gcol - gcol.T
            dec.append(jnp.where(strict, jnp.exp(jnp.where(strict, diff, 0.0)), 0.0))
        nmat = [beta[h] * kk[h] * dec[h] for h in hs]
        a_qk = [(qk[h] * (dec[h] + eye)).astype(BF16) for h in hs]

        mp = [-(nmat[h] * diag_mask) for h in hs]
        x = [eye + mp[h] for h in hs]
        for _ in range(GDN_INV_BLOCK.bit_length() - 2):
            mpb = [mp[h].astype(BF16) for h in hs]
            mp = [_dot(mpb[h], mpb[h]) for h in hs]
            x = [x[h] + _dot(x[h].astype(BF16), mp[h].astype(BF16)) for h in hs]
        nb = [nmat[h].astype(BF16) for h in hs]
        for lm in level_masks:
            xb = [x[h].astype(BF16) for h in hs]
            cx = [_dot(nb[h] * lm, xb[h]).astype(BF16) for h in hs]
            x = [x[h] - _dot(xb[h], cx[h]) for h in hs]
        xb = [x[h].astype(BF16) for h in hs]

        g_cum = [jnp.exp(gam[h]) for h in hs]
        w = [_dot(xb[h], ((beta[h] * g_cum[h]) * kh[h]).astype(BF16)).astype(BF16) for h in hs]
        u = [_dot(xb[h], (beta[h] * vh[h]).astype(BF16)) for h in hs]
        g_last = [gam[h][c - 1:c, :] for h in hs]
        q_dec = [(qh[h] * g_cum[h]).astype(BF16) for h in hs]
        k_dec = [(kh[h] * jnp.exp(g_last[h] - gam[h])).astype(BF16) for h in hs]

        state = [st_ref[h] for h in hs]
        sb = [state[h].astype(BF16) for h in hs]
        db = [(u[h] - _dot(w[h], sb[h])).astype(BF16) for h in hs]
        out = [_dot(q_dec[h], sb[h]) + _dot(a_qk[h], db[h]) for h in hs]
        for h in hs:
            st_ref[h] = jnp.exp(g_last[h]) * state[h] + _dot_tn(k_dec[h], db[h])
        for h in hs:
            normed = (out[h] * lax.rsqrt(jnp.mean(out[h] * out[h], axis=-1, keepdims=True) + RMS_EPS)
                      * ng_ref[...])
            o_ref[0, rows, cols[h]] = (normed * _silu(z_ref[0, rows, cols[h]])).astype(BF16)
        return carry

    lax.fori_loop(0, n_steps, step, 0)


def _gdn(qkv, z, ab, a_log, dt_bias, norm_g):
    bsz, seq, _ = qkv.shape
    heads = B_HEADS
    gw = heads * B_KEY_DIM
    pad_lanes = lambda v: jnp.pad(v.reshape(1, -1), ((0, 0), (0, LANES - v.size)))
    kern = functools.partial(_gdn_kernel, n_steps=seq // GDN_CHUNK, heads=heads)
    col_block = lambda off: (lambda b: (b, 0, off))
    const2 = lambda b: (0, 0)
    return pl.pallas_call(
        kern,
        grid=(bsz,),
        in_specs=[pl.BlockSpec((1, seq, gw), col_block(0)),
                  pl.BlockSpec((1, seq, gw), col_block(1)),
                  pl.BlockSpec((1, seq, gw), col_block(2)),
                  pl.BlockSpec((1, seq, gw), col_block(0)),
                  pl.BlockSpec((1, seq, LANES), lambda b: (b, 0, 0)),
                  pl.BlockSpec((1, LANES), const2),
                  pl.BlockSpec((1, LANES), const2),
                  pl.BlockSpec((1, B_VAL_DIM), const2)],
        out_specs=pl.BlockSpec((1, seq, gw), col_block(0)),
        out_shape=jax.ShapeDtypeStruct((bsz, seq, B_HEADS * B_VAL_DIM), BF16),
        scratch_shapes=[pltpu.VMEM((heads, B_KEY_DIM, B_VAL_DIM), F32)],
        compiler_params=_params("arbitrary"),
        name="gdn",
    )(qkv, qkv, qkv, z, ab, pad_lanes(a_log), pad_lanes(dt_bias), norm_g.reshape(1, B_VAL_DIM))


def _outproj_kernel(x_ref, mod_ref, attn_ref, dn_ref, wo_ref, g_ref, b_ref, o_ref, *, n_out_chunks):
    gate = mod_ref[0, 5:6, :]
    rows_per = x_ref.shape[1] // n_out_chunks
    for r in range(n_out_chunks):
        rows = slice(r * rows_per, (r + 1) * rows_per)
        mixed = jnp.concatenate([attn_ref[0, rows, :], dn_ref[0, rows, :]], axis=1)
        y = DEEPNORM_ALPHA * x_ref[0, rows, :] + gate * _dot(mixed, wo_ref[...])
        o_ref[0, rows, :] = _layer_norm(y, g_ref[...], b_ref[...])


def _outproj(x, mod, attn, dn, w_out, ln_g, ln_b):
    bsz, seq, d = x.shape
    tm = min(OUT_TOKENS, seq)
    tok = lambda b, i: (b, i, 0)
    const2 = lambda b, i: (0, 0)
    kern = functools.partial(_outproj_kernel, n_out_chunks=max(1, tm // OUT_ROWS))
    return pl.pallas_call(
        kern,
        grid=(bsz, seq // tm),
        in_specs=[pl.BlockSpec((1, tm, d), tok),
                  pl.BlockSpec((1, N_MOD, d), lambda b, i: (b, 0, 0)),
                  pl.BlockSpec((1, tm, attn.shape[2]), tok),
                  pl.BlockSpec((1, tm, dn.shape[2]), tok),
                  pl.BlockSpec(w_out.shape, const2),
                  pl.BlockSpec((1, d), const2),
                  pl.BlockSpec((1, d), const2)],
        out_specs=pl.BlockSpec((1, tm, d), tok),
        out_shape=jax.ShapeDtypeStruct((bsz, seq, d), F32),
        compiler_params=_params("arbitrary", "arbitrary"),
        name="outproj",
    )(x, mod, attn, dn, w_out.astype(BF16), ln_g.reshape(1, d), ln_b.reshape(1, d))


def kernel(x, c, positions, w_ada, b_ada, ffn1_w1, ffn1_w3, ffn1_w2, ln1_g, ln1_b, w_in, conv_w, a_log, dt_bias, dn_norm_g, w_out, ln2_g, ln2_b, ffn2_w1, ffn2_w3, ffn2_w2, ln3_g, ln3_b):
    bsz, seq, d = x.shape
    pos_f = positions.astype(F32).reshape(bsz, 1, seq)
    for layer in range(w_ada.shape[0]):
        mod = _adaln(c, w_ada[layer], b_ada[layer]).reshape(bsz, N_MOD, d)
        x = _ffn_block(x, mod, ffn1_w1[layer], ffn1_w3[layer], ffn1_w2[layer], ln1_g[layer], ln1_b[layer],
                       mod_base=0)
        qt, qit, wt, k, ki, vt, qkv, z, ab = _proj(x, mod, pos_f, w_in[layer], conv_w[layer])
        attn = _dsa(qt, qit, wt, k, ki, vt)
        dn = _gdn(qkv, z, ab, a_log[layer], dt_bias[layer], dn_norm_g[layer])
        x = _outproj(x, mod, attn, dn, w_out[layer], ln2_g[layer], ln2_b[layer])
        x = _ffn_block(x, mod, ffn2_w1[layer], ffn2_w3[layer], ffn2_w2[layer], ln3_g[layer], ln3_b[layer],
                       mod_base=6)
    return x
```
